```python
import jax
import jax.numpy as jnp
from jax import lax
import numpy as np

D_MODEL = 1024
BATCH = 2
SEQ = 8192
DEPTH = 4

CHUNK = 64
RWKV_HEADS = 8
RWKV_HEAD_DIM = 64
RWKV_WIDTH = RWKV_HEADS * RWKV_HEAD_DIM
DECAY_RANK = 64
ICLR_RANK = 64
GATE_RANK = 128
VRES_RANK = 32
ATT_HEADS = 8
ATT_HEAD_DIM = 64
ATT_WIDTH = ATT_HEADS * ATT_HEAD_DIM
LEFT_CHUNKS = 8
BAND_CHUNKS = LEFT_CHUNKS + 1
BAND = BAND_CHUNKS * CHUNK
REL_MIN = -(CHUNK - 1)
REL_MAX = 128
N_REL = REL_MAX - REL_MIN + 1
MEM_TOKENS = 256
MEM_HEADS = 4
MEM_HEAD_DIM = 128
MEM_WIDTH = MEM_HEADS * MEM_HEAD_DIM
N_BRANCHES = 3
D_FF = 2816
RMS_EPS = 1e-6
GN_EPS = 64e-5
L2_EPS = 1e-12
NEG_INF = -1e30

RWKV_IN = 3 * RWKV_WIDTH + DECAY_RANK + ICLR_RANK + GATE_RANK
ATT_IN = 3 * ATT_WIDTH
MEM_IN = MEM_WIDTH
D_IN = RWKV_IN + ATT_IN + MEM_IN
RWKV_SPLITS = (RWKV_WIDTH, 2 * RWKV_WIDTH, 3 * RWKV_WIDTH,
               3 * RWKV_WIDTH + DECAY_RANK, 3 * RWKV_WIDTH + DECAY_RANK + ICLR_RANK)

kernel_name = 'hybrid_rwkv7_chunkattn_memory_macaron'


def rms_norm(x, gain):
    xf = x.astype(jnp.float32)
    y = xf * lax.rsqrt(jnp.mean(xf * xf, axis=-1, keepdims=True) + RMS_EPS)
    return (y * gain.astype(jnp.float32)).astype(x.dtype)


def swiglu(x, w_in, w_out):
    gate, up = jnp.split(x @ w_in, 2, axis=-1)
    return (jax.nn.silu(gate) * up) @ w_out


def shift_one(u):
    return jnp.pad(u, ((0, 0), (1, 0), (0, 0)))[:, :-1]


def split_heads(u, n_heads):
    return u.reshape(u.shape[0], u.shape[1], n_heads, -1)


def wkv7_scan(r, w, k, v, kk, b):
    def step(state, inp):
        r_t, w_t, k_t, v_t, kk_t, b_t = inp
        sa = jnp.einsum('bhvk,bhk->bhv', state, kk_t)
        state = (state * w_t[:, :, None, :]
                 - sa[..., None] * b_t[:, :, None, :]
                 + v_t[..., None] * k_t[:, :, None, :])
        return state, jnp.einsum('bhvk,bhk->bhv', state, r_t)
    xs = tuple(jnp.moveaxis(t, 1, 0) for t in (r, w, k, v, kk, b))
    bsz, _, nh, n = r.shape
    s0 = jnp.zeros((bsz, nh, n, n), jnp.float32)
    _, ys = lax.scan(step, s0, xs)
    return jnp.moveaxis(ys, 0, 1)


def rwkv7_time_mix(p, h, mu, w0, decay_b, a0, iclr_b, gate_b, k_k, k_a, r_k, gn_g, gn_b,
                   v_first, v0, vres_a, vres_b, use_vres):
    f32 = jnp.float32
    bsz, seq = p.shape[0], p.shape[1]
    p = p + (shift_one(p) - p) * mu
    r, k, v, xw, xa, xg = jnp.split(p, RWKV_SPLITS, axis=-1)
    w_log = -jax.nn.softplus(-(w0 + jnp.tanh(xw) @ decay_b).astype(f32)) - 0.5
    decay = jnp.exp(-jnp.exp(w_log))
    a = jax.nn.sigmoid((a0 + xa @ iclr_b).astype(f32))
    g = jax.nn.sigmoid(xg) @ gate_b
    if use_vres:
        v = v + (v_first - v) * jax.nn.sigmoid(v0 + (h @ vres_a) @ vres_b)
    else:
        v_first = v
    r, k, v = (t.astype(f32) for t in (r, k, v))
    kk = split_heads(k * k_k, RWKV_HEADS)
    kk = kk / jnp.maximum(jnp.linalg.norm(kk, axis=-1, keepdims=True), L2_EPS)
    k = k * (1.0 + (a - 1.0) * k_a)
    rh = split_heads(r, RWKV_HEADS)
    kh = split_heads(k, RWKV_HEADS)
    vh = split_heads(v, RWKV_HEADS)
    ah = split_heads(a, RWKV_HEADS)
    y = wkv7_scan(rh, split_heads(decay, RWKV_HEADS), kh, vh, kk, kk * ah)
    mean = jnp.mean(y, axis=-1, keepdims=True)
    var = jnp.mean(jnp.square(y - mean), axis=-1, keepdims=True)
    y = ((y - mean) * lax.rsqrt(var + GN_EPS)).reshape(bsz, seq, RWKV_WIDTH) * gn_g + gn_b
    bonus = jnp.sum(rh * kh * r_k, axis=-1, keepdims=True) * vh
    y = (y + bonus.reshape(bsz, seq, RWKV_WIDTH)) * g.astype(f32)
    return y.astype(p.dtype), v_first


def chunk_band(t):
    bsz, seq, nh, dh = t.shape
    nc = seq // CHUNK
    tc = t.reshape(bsz, nc, CHUNK, nh, dh)
    tp = jnp.pad(tc, ((0, 0), (LEFT_CHUNKS, 0), (0, 0), (0, 0), (0, 0)))
    band = jnp.stack([tp[:, j:j + nc] for j in range(BAND_CHUNKS)], axis=2)
    return band.reshape(bsz, nc, BAND, nh, dh)


def chunked_rel_attention(q, k, v, rel_table):
    bsz, seq, nh, dh = q.shape
    nc = seq // CHUNK
    qc = q.reshape(bsz, nc, CHUNK, nh, dh)
    kb = chunk_band(k)
    vb = chunk_band(v)
    scores = jnp.einsum('bnqhd,bnkhd->bnhqk', qc, kb,
                        preferred_element_type=jnp.float32) * (dh ** -0.5)
    q_off = jnp.arange(CHUNK)[:, None]
    k_off = jnp.arange(BAND)[None, :]
    dist = LEFT_CHUNKS * CHUNK + q_off - k_off
    bias = rel_table[:, jnp.clip(dist, REL_MIN, REL_MAX) - REL_MIN]
    key_chunk = jnp.arange(nc)[:, None] - LEFT_CHUNKS + jnp.arange(BAND_CHUNKS)[None, :]
    valid = jnp.repeat(key_chunk >= 0, CHUNK, axis=1)
    scores = scores + bias.astype(jnp.float32)[None, None]
    scores = jnp.where(valid[None, :, None, None, :], scores, NEG_INF)
    probs = jax.nn.softmax(scores, axis=-1).astype(v.dtype)
    out = jnp.einsum('bnhqk,bnkhd->bnqhd', probs, vb)
    return out.reshape(bsz, seq, nh * dh)


def memory_attention(q, mk, mv):
    bsz, seq, nh, dh = q.shape
    scores = jnp.einsum('bshd,bmhd->bhsm', q, mk,
                        preferred_element_type=jnp.float32) * (dh ** -0.5)
    probs = jax.nn.softmax(scores, axis=-1).astype(mv.dtype)
    out = jnp.einsum('bhsm,bmhd->bshd', probs, mv)
    return out.reshape(bsz, seq, nh * dh)


def setup_inputs(seed: int = 0) -> dict:
    key = jax.random.key(seed)
    keys = iter(jax.random.split(key, 48))
    f32 = jnp.float32
    L = DEPTH
    D = D_MODEL

    def nrm(shape, scale):
        return jax.random.normal(next(keys), shape, f32) * scale

    def gain(shape):
        return 1.0 + nrm(shape, 0.02)

    def unif(shape, lo, hi):
        return jax.random.uniform(next(keys), shape, f32, lo, hi)

    return {
        'x': nrm((BATCH, SEQ, D), 1.0),
        'mem': nrm((BATCH, MEM_TOKENS, D), 1.0),
        'norm_ffn1': gain((L, D)),
        'ffn1_w_in': nrm((L, D, 2 * D_FF), D ** -0.5),
        'ffn1_w_out': nrm((L, D_FF, D), D_FF ** -0.5),
        'norm_mix': gain((L, D)),
        'w_in': nrm((L, D, D_IN), D ** -0.5),
        'shift_mu': unif((L, RWKV_IN), 0.0, 1.0),
        'decay_w0': unif((L, RWKV_WIDTH), -3.0, 0.0),
        'decay_lora_b': nrm((L, DECAY_RANK, RWKV_WIDTH), 0.5 * DECAY_RANK ** -0.5),
        'iclr_a0': nrm((L, RWKV_WIDTH), 0.5),
        'iclr_lora_b': nrm((L, ICLR_RANK, RWKV_WIDTH), 0.5 * ICLR_RANK ** -0.5),
        'gate_lora_b': nrm((L, GATE_RANK, RWKV_WIDTH), GATE_RANK ** -0.5),
        'rwkv_k_k': 0.85 + nrm((L, RWKV_WIDTH), 0.05),
        'rwkv_k_a': 1.0 + nrm((L, RWKV_WIDTH), 0.05),
        'rwkv_r_k': nrm((L, RWKV_HEADS, RWKV_HEAD_DIM), 0.1),
        'rwkv_gn_g': gain((L, RWKV_WIDTH)),
        'rwkv_gn_b': nrm((L, RWKV_WIDTH), 0.02),
        'vres_v0': nrm((L - 1, RWKV_WIDTH), 0.5),
        'vres_lora_a': nrm((L - 1, D, VRES_RANK), D ** -0.5),
        'vres_lora_b': nrm((L - 1, VRES_RANK, RWKV_WIDTH), VRES_RANK ** -0.5),
        'att_q_norm': gain((L, ATT_HEAD_DIM)),
        'att_k_norm': gain((L, ATT_HEAD_DIM)),
        'att_rel_bias': nrm((L, ATT_HEADS, N_REL), 0.5),
        'norm_mem': gain((L, D)),
        'mem_w_kv': nrm((L, D, 2 * MEM_WIDTH), D ** -0.5),
        'mem_q_norm': gain((L, MEM_HEAD_DIM)),
        'mem_k_norm': gain((L, MEM_HEAD_DIM)),
        'w_branch_rwkv': nrm((L, RWKV_WIDTH, D), RWKV_WIDTH ** -0.5),
        'w_branch_att': nrm((L, ATT_WIDTH, D), ATT_WIDTH ** -0.5),
        'w_branch_mem': nrm((L, MEM_WIDTH, D), MEM_WIDTH ** -0.5),
        'w_gate': nrm((L, D, N_BRANCHES * D), D ** -0.5),
        'b_gate': nrm((L, N_BRANCHES * D), 0.02),
        'w_out': nrm((L, D, D), D ** -0.5),
        'norm_ffn2': gain((L, D)),
        'ffn2_w_in': nrm((L, D, 2 * D_FF), D ** -0.5),
        'ffn2_w_out': nrm((L, D_FF, D), D_FF ** -0.5),
    }


def reference(x, mem, norm_ffn1, ffn1_w_in, ffn1_w_out, norm_mix, w_in, shift_mu, decay_w0,
              decay_lora_b, iclr_a0, iclr_lora_b, gate_lora_b, rwkv_k_k, rwkv_k_a, rwkv_r_k,
              rwkv_gn_g, rwkv_gn_b, vres_v0, vres_lora_a, vres_lora_b, att_q_norm, att_k_norm,
              att_rel_bias, norm_mem, mem_w_kv, mem_q_norm, mem_k_norm, w_branch_rwkv,
              w_branch_att, w_branch_mem, w_gate, b_gate, w_out, norm_ffn2, ffn2_w_in, ffn2_w_out):
    v_first = None
    for l in range(DEPTH):
        x = x + 0.5 * swiglu(rms_norm(x, norm_ffn1[l]), ffn1_w_in[l], ffn1_w_out[l])

        h = rms_norm(x, norm_mix[l])
        proj = h @ w_in[l]
        p_rwkv = proj[..., :RWKV_IN]
        p_att = proj[..., RWKV_IN:RWKV_IN + ATT_IN]
        p_mem = proj[..., RWKV_IN + ATT_IN:]

        use_vres = l > 0
        li = max(l - 1, 0)
        y_rwkv, v_first = rwkv7_time_mix(
            p_rwkv, h, shift_mu[l], decay_w0[l], decay_lora_b[l], iclr_a0[l], iclr_lora_b[l],
            gate_lora_b[l], rwkv_k_k[l], rwkv_k_a[l], rwkv_r_k[l], rwkv_gn_g[l], rwkv_gn_b[l],
            v_first, vres_v0[li], vres_lora_a[li], vres_lora_b[li], use_vres)

        aq, ak, av = jnp.split(p_att, 3, axis=-1)
        aq = rms_norm(split_heads(aq, ATT_HEADS), att_q_norm[l])
        ak = rms_norm(split_heads(ak, ATT_HEADS), att_k_norm[l])
        av = split_heads(av, ATT_HEADS)
        y_att = chunked_rel_attention(aq, ak, av, att_rel_bias[l])

        mkv = rms_norm(mem, norm_mem[l]) @ mem_w_kv[l]
        mk, mv = jnp.split(mkv, 2, axis=-1)
        mk = rms_norm(split_heads(mk, MEM_HEADS), mem_k_norm[l])
        mv = split_heads(mv, MEM_HEADS)
        mq = rms_norm(split_heads(p_mem, MEM_HEADS), mem_q_norm[l])
        y_mem = memory_attention(mq, mk, mv)

        g_rwkv, g_att, g_mem = jnp.split(jax.nn.sigmoid(h @ w_gate[l] + b_gate[l]), N_BRANCHES, axis=-1)
        merged = (g_rwkv * (y_rwkv @ w_branch_rwkv[l])
                  + g_att * (y_att @ w_branch_att[l])
                  + g_mem * (y_mem @ w_branch_mem[l]))
        x = x + merged @ w_out[l]

        x = x + 0.5 * swiglu(rms_norm(x, norm_ffn2[l]), ffn2_w_in[l], ffn2_w_out[l])
    return x
```

```python
import functools

import jax
import jax.numpy as jnp
from jax import lax
from jax.experimental import pallas as pl
from jax.experimental.pallas import tpu as pltpu

F32 = jnp.float32
BF16 = jnp.bfloat16

V7X_VMEM_LIMIT_BYTES = 56 * 1024 * 1024

D_MODEL = 1024
DEPTH = 4
CHUNK = 64
RWKV_HEADS = 8
RWKV_HEAD_DIM = 64
RWKV_WIDTH = RWKV_HEADS * RWKV_HEAD_DIM
DECAY_RANK = 64
ICLR_RANK = 64
GATE_RANK = 128
VRES_RANK = 32
ATT_HEADS = 8
ATT_HEAD_DIM = 64
ATT_WIDTH = ATT_HEADS * ATT_HEAD_DIM
LEFT_CHUNKS = 8
REL_MIN = -(CHUNK - 1)
REL_MAX = 128
MEM_HEADS = 4
MEM_HEAD_DIM = 128
MEM_WIDTH = MEM_HEADS * MEM_HEAD_DIM
D_FF = 2816
RMS_EPS = 1e-6
GN_EPS = 64e-5
L2_EPS = 1e-12
NEG_INF = -1e30

RWKV_IN = 3 * RWKV_WIDTH + DECAY_RANK + ICLR_RANK + GATE_RANK
LANES = 128
RWKV_EXT = RWKV_IN + LANES
ATT_IN = 3 * ATT_WIDTH

TM_FFN = 512
TM_PROJ = 512
TM_MERGE = 512
TT_RWKV = 512
TQ_ATT = 256
KW_ATT = 3 * TQ_ATT
TM_MEM = 512


def _dot(a, b):
    return jnp.dot(a, b, preferred_element_type=F32)


def _dot_nt(a, b):
    return lax.dot_general(a, b, (((1,), (1,)), ((), ())), preferred_element_type=F32)


def _dot_tn(a, b):
    return lax.dot_general(a, b, (((0,), (0,)), ((), ())), preferred_element_type=F32)


def _rms(x, gain):
    return x * lax.rsqrt(jnp.mean(x * x, axis=-1, keepdims=True) + RMS_EPS) * gain


def _group_dot(z, g):
    hi = z.astype(BF16)
    lo = (z - hi.astype(F32)).astype(BF16)
    return _dot(hi, g) + _dot(lo, g)


def _const_spec(shape):
    nd = len(shape)
    return pl.BlockSpec(shape, lambda *_: (0,) * nd, pipeline_mode=pl.Buffered(1))


def _params(n_axes):
    return pltpu.CompilerParams(dimension_semantics=("arbitrary",) * n_axes,
                                vmem_limit_bytes=V7X_VMEM_LIMIT_BYTES)


def _ffn_kernel(x_ref, g_ref, win_ref, wout_ref, o_ref):
    x = x_ref[...]
    h = _rms(x, g_ref[...]).astype(BF16)
    gu = _dot(h, win_ref[...])
    gate = gu[:, :D_FF]
    up = gu[:, D_FF:]
    act = (gate * jax.nn.sigmoid(gate) * up).astype(BF16)
    o_ref[...] = x + 0.5 * _dot(act, wout_ref[...])


def _ffn(x, gain, w_in, w_out):
    t, d = x.shape
    return pl.pallas_call(
        _ffn_kernel,
        grid=(t // TM_FFN,),
        in_specs=[pl.BlockSpec((TM_FFN, d), lambda i: (i, 0)),
                  _const_spec((1, d)),
                  _const_spec(w_in.shape),
                  _const_spec(w_out.shape)],
        out_specs=pl.BlockSpec((TM_FFN, d), lambda i: (i, 0)),
        out_shape=jax.ShapeDtypeStruct((t, d), F32),
        compiler_params=_params(1),
        name="ffn",
    )(x, gain.reshape(1, d), w_in, w_out)


def _proj_kernel(x_ref, g_ref, w1_ref, w2_ref, w3_ref, qkg_ref, mqg_ref, g64_ref, g128_ref,
                 p1_ref, qkv_ref, mq_ref):
    h = _rms(x_ref[...], g_ref[...]).astype(BF16)
    p1_ref[...] = _dot(h, w1_ref[...])
    qkv = _dot(h, w2_ref[...])
    qk = qkv[:, :2 * ATT_WIDTH]
    ms = _group_dot(qk * qk, g64_ref[...])
    qkv_ref[:, :2 * ATT_WIDTH] = (qk * lax.rsqrt(ms + RMS_EPS) * qkg_ref[...]).astype(BF16)
    qkv_ref[:, 2 * ATT_WIDTH:] = qkv[:, 2 * ATT_WIDTH:].astype(BF16)
    mq = _dot(h, w3_ref[...])
    ms = _group_dot(mq * mq, g128_ref[...])
    mq_ref[...] = (mq * lax.rsqrt(ms + RMS_EPS) * mqg_ref[...]).astype(BF16)


def _proj(x, gain, w1, w2, w3, qk_gain, mq_gain, g64, g128):
    t, d = x.shape
    tm = TM_PROJ
    return pl.pallas_call(
        _proj_kernel,
        grid=(t // tm,),
        in_specs=[pl.BlockSpec((tm, d), lambda i: (i, 0)),
                  _const_spec((1, d)),
                  _const_spec(w1.shape), _const_spec(w2.shape), _const_spec(w3.shape),
                  _const_spec(qk_gain.shape), _const_spec(mq_gain.shape),
                  _const_spec(g64.shape), _const_spec(g128.shape)],
        out_specs=[pl.BlockSpec((tm, RWKV_EXT), lambda i: (i, 0)),
                   pl.BlockSpec((tm, ATT_IN), lambda i: (i, 0)),
                   pl.BlockSpec((tm, MEM_WIDTH), lambda i: (i, 0))],
        out_shape=[jax.ShapeDtypeStruct((t, RWKV_EXT), F32),
                   jax.ShapeDtypeStruct((t, ATT_IN), BF16),
                   jax.ShapeDtypeStruct((t, MEM_WIDTH), BF16)],
        compiler_params=_params(1),
        name="proj",
    )(x, gain.reshape(1, d), w1, w2, w3, qk_gain, mq_gain, g64, g128)


def _rwkv_kernel(*refs, use_vres):
    if use_vres:
        (p_ref, vf_ref, mu_ref, w0_ref, a0_ref, lora_ref, gateb_ref, kk_ref, ka_ref, rk_ref,
         gng_ref, gnb_ref, v0_ref, vresb_ref, g64_ref, ones64_ref, tri_ref,
         y_ref,
         carry_ref, state_ref, rt_ref, at_ref, bt_ref, kt_ref, v_ref, w_ref, ys_ref) = refs
        vfo_ref = None
    else:
        (p_ref, mu_ref, w0_ref, a0_ref, lora_ref, gateb_ref, kk_ref, ka_ref, rk_ref,
         gng_ref, gnb_ref, g64_ref, ones64_ref, tri_ref,
         y_ref, vfo_ref,
         carry_ref, state_ref, rt_ref, at_ref, bt_ref, kt_ref, v_ref, w_ref, ys_ref) = refs
    tt = p_ref.shape[0]
    n_chunks = tt // CHUNK
    w_ = RWKV_WIDTH

    @pl.when(pl.program_id(1) == 0)
    def _():
        carry_ref[...] = jnp.zeros_like(carry_ref)
        state_ref[...] = jnp.zeros_like(state_ref)

    p = p_ref[...]
    row = lax.broadcasted_iota(jnp.int32, p.shape, 0)
    prev = jnp.where(row == 0, carry_ref[0:1, :], pltpu.roll(p, 1, axis=0))
    carry_ref[0:1, :] = p[tt - 1:tt, :]
    xs = p + (prev - p) * mu_ref[...]

    r = xs[:, 0:w_]
    k = xs[:, w_:2 * w_]
    v = xs[:, 2 * w_:3 * w_]
    lw = xs[:, 3 * w_:3 * w_ + LANES]
    xg = xs[:, 3 * w_ + LANES:3 * w_ + 2 * LANES]

    lane = lax.broadcasted_iota(jnp.int32, lw.shape, 1)
    z = jnp.where(lane < DECAY_RANK, jnp.tanh(lw), lw).astype(BF16)
    lo = _dot(z, lora_ref[...])
    logw = -jnp.exp(F32(-0.5)) * jax.nn.sigmoid(w0_ref[...] + lo[:, :w_])
    a = jax.nn.sigmoid(a0_ref[...] + lo[:, w_:])
    g = _dot(jax.nn.sigmoid(xg).astype(BF16), gateb_ref[...])

    if use_vres:
        hv = xs[:, RWKV_IN:RWKV_EXT].astype(BF16)
        mix = jax.nn.sigmoid(v0_ref[...] + _dot(hv, vresb_ref[...]))
        v = v + (vf_ref[...] - v) * mix
    else:
        vfo_ref[...] = v

    kk = k * kk_ref[...]
    ss = _group_dot(kk * kk, ones64_ref[...])
    kk = kk * lax.rsqrt(jnp.maximum(ss, L2_EPS * L2_EPS))
    k2 = k * (1.0 + (a - 1.0) * ka_ref[...])
    bvec = kk * a

    tri = tri_ref[...]
    hi = logw.astype(BF16)
    r1 = logw - hi.astype(F32)
    mid = r1.astype(BF16)
    low = (r1 - mid.astype(F32)).astype(BF16)
    cum = _dot(tri, hi) + _dot(tri, mid) + _dot(tri, low)
    wcum = jnp.exp(cum)
    inv_w = jnp.exp(-cum)
    w_ref[...] = wcum
    rt_ref[...] = (r * wcum).astype(BF16)
    at_ref[...] = (-kk * jnp.exp(cum - logw)).astype(BF16)
    bt_ref[...] = (bvec * inv_w).astype(BF16)
    kt_ref[...] = (k2 * inv_w).astype(BF16)
    v_ref[...] = v.astype(BF16)

    li = lax.broadcasted_iota(jnp.int32, (CHUNK, CHUNK), 0)
    mi = lax.broadcasted_iota(jnp.int32, (CHUNK, CHUNK), 1)
    strict = mi < li
    incl = mi <= li
    eye = (mi == li).astype(F32)

    def chunk_body(c, carry):
        rows = pl.ds(pl.multiple_of(c * CHUNK, CHUNK), CHUNK)
        tail = pl.ds(pl.multiple_of(c * CHUNK + CHUNK - 8, 8), 8)
        for hp in range(RWKV_HEADS // 2):
            cols = slice(hp * LANES, (hp + 1) * LANES)
            rt2, at2, bt2, kt2, v2 = (ref[rows, cols] for ref in (rt_ref, at_ref, bt_ref, kt_ref, v_ref))
            wl2 = w_ref[tail, cols][7:8, :]
            ys = []
            for sub in range(2):
                h = 2 * hp + sub
                hc = slice(sub * RWKV_HEAD_DIM, (sub + 1) * RWKV_HEAD_DIM)
                rt, at, bt, kt, vv = rt2[:, hc], at2[:, hc], bt2[:, hc], kt2[:, hc], v2[:, hc]
                a_ab = jnp.where(strict, _dot_nt(at, bt), 0.0)
                a_ak = jnp.where(strict, _dot_nt(at, kt), 0.0)
                a_rb = jnp.where(incl, _dot_nt(rt, bt), 0.0)
                a_rk = jnp.where(incl, _dot_nt(rt, kt), 0.0)
                tinv = eye + a_ab
                pw = a_ab
                for _ in range(5):
                    pwb = pw.astype(BF16)
                    pw = _dot(pwb, pwb)
                    tinv = tinv + _dot(pw.astype(BF16), tinv.astype(BF16))
                s0 = state_ref[h]
                s0b = s0.astype(BF16)
                rhs = _dot_nt(at, s0b) + _dot(a_ak.astype(BF16), vv)
                u = _dot(tinv.astype(BF16), rhs.astype(BF16))
                ub = u.astype(BF16)
                y = _dot_nt(rt, s0b) + _dot(a_rb.astype(BF16), ub) + _dot(a_rk.astype(BF16), vv)
                state_ref[h] = (s0 + _dot_tn(ub, bt) + _dot_tn(vv, kt)) * wl2[:, hc]
                ys.append(y)
            ys_ref[rows, cols] = jnp.concatenate(ys, axis=-1)
        return carry

    lax.fori_loop(0, n_chunks, chunk_body, 0)

    y = ys_ref[...]
    mean = _group_dot(y, g64_ref[...])
    yc = y - mean
    var = _group_dot(yc * yc, g64_ref[...])
    yn = yc * lax.rsqrt(var + GN_EPS) * gng_ref[...] + gnb_ref[...]
    bonus = _group_dot(r * k2 * rk_ref[...], ones64_ref[...]) * v
    y_ref[...] = ((yn + bonus) * g).astype(BF16)


def _rwkv(p1, v_first, seq, prm, use_vres):
    t = p1.shape[0]
    tt = TT_RWKV
    nt = seq // tt
    w_ = RWKV_WIDTH
    tok = lambda width: pl.BlockSpec((tt, width), lambda b, i: (b * nt + i, 0))
    vec = _const_spec((1, w_))
    in_specs = [tok(RWKV_EXT)]
    args = [p1]
    if use_vres:
        in_specs.append(tok(w_))
        args.append(v_first)
    in_specs += [_const_spec((1, RWKV_EXT)), vec, vec, _const_spec((LANES, 2 * w_)),
                 _const_spec((GATE_RANK, w_)), vec, vec, vec, vec, vec]
    args += [prm["mu"], prm["w0"], prm["a0"], prm["lora"], prm["gate_b"], prm["k_k"], prm["k_a"],
             prm["r_k"], prm["gn_g"], prm["gn_b"]]
    if use_vres:
        in_specs += [vec, _const_spec((LANES, w_))]
        args += [prm["v0"], prm["vres_b"]]
    in_specs += [_const_spec((w_, w_)), _const_spec((w_, w_)), _const_spec((tt, tt))]
    args += [prm["g64"], prm["ones64"], prm["tri"]]
    out_specs = [tok(w_)]
    out_shape = [jax.ShapeDtypeStruct((t, w_), BF16)]
    if not use_vres:
        out_specs.append(tok(w_))
        out_shape.append(jax.ShapeDtypeStruct((t, w_), F32))
    scratch = [pltpu.VMEM((8, RWKV_EXT), F32),
               pltpu.VMEM((RWKV_HEADS, RWKV_HEAD_DIM, RWKV_HEAD_DIM), F32),
               pltpu.VMEM((tt, w_), BF16), pltpu.VMEM((tt, w_), BF16), pltpu.VMEM((tt, w_), BF16),
               pltpu.VMEM((tt, w_), BF16), pltpu.VMEM((tt, w_), BF16),
               pltpu.VMEM((tt, w_), F32), pltpu.VMEM((tt, w_), F32)]
    out = pl.pallas_call(
        functools.partial(_rwkv_kernel, use_vres=use_vres),
        grid=(t // seq, nt),
        in_specs=in_specs,
        out_specs=out_specs,
        out_shape=out_shape,
        scratch_shapes=scratch,
        compiler_params=_params(2),
        name="rwkv",
    )(*args)
    if use_vres:
        return out[0], v_first
    return out[0], out[1]


def _att_kernel(q_ref, k0_ref, k1_ref, k2_ref, v0_ref, v1_ref, v2_ref, bias_ref, o_ref):
    i = pl.program_id(1)
    tq = q_ref.shape[0]
    kw = 3 * tq
    col = lax.broadcasted_iota(jnp.int32, (tq, kw), 1)
    valid = col >= (2 - i) * tq
    lane = lax.broadcasted_iota(jnp.int32, (1, LANES), 1)
    zero = jnp.zeros((), BF16)
    for hp in range(ATT_HEADS // 2):
        cols = slice(hp * LANES, (hp + 1) * LANES)
        q2 = q_ref[:, cols]
        k2 = jnp.concatenate([k0_ref[:, cols], k1_ref[:, cols], k2_ref[:, cols]], axis=0)
        v2 = jnp.concatenate([v0_ref[:, cols], v1_ref[:, cols], v2_ref[:, cols]], axis=0)
        out = jnp.zeros((tq, LANES), F32)
        for sub in range(2):
            head_lanes = (lane >= sub * ATT_HEAD_DIM) & (lane < (sub + 1) * ATT_HEAD_DIM)
            s = _dot_nt(jnp.where(head_lanes, q2, zero), k2) + bias_ref[2 * hp + sub]
            s = jnp.where(valid, s, NEG_INF)
            e = jnp.exp(s - jnp.max(s, axis=-1, keepdims=True))
            denom = jnp.sum(e, axis=-1, keepdims=True)
            pv = _dot(e.astype(BF16), jnp.where(head_lanes, v2, zero))
            out = out + pv / denom
        o_ref[:, cols] = out.astype(BF16)


def _att(qkv, bias, seq):
    t = qkv.shape[0]
    tq = TQ_ATT
    nq = seq // tq
    blk = lambda col, back: pl.BlockSpec(
        (tq, ATT_WIDTH), lambda b, i: (b * nq + jnp.maximum(i - back, 0), col))
    return pl.pallas_call(
        _att_kernel,
        grid=(t // seq, nq),
        in_specs=[blk(0, 0), blk(1, 2), blk(1, 1), blk(1, 0), blk(2, 2), blk(2, 1), blk(2, 0),
                  _const_spec(bias.shape)],
        out_specs=pl.BlockSpec((tq, ATT_WIDTH), lambda b, i: (b * nq + i, 0)),
        out_shape=jax.ShapeDtypeStruct((t, ATT_WIDTH), BF16),
        compiler_params=_params(2),
        name="band_att",
    )(qkv, qkv, qkv, qkv, qkv, qkv, qkv, bias)


def _att_bias(rel_table):
    qi = jnp.arange(TQ_ATT)[:, None]
    kj = jnp.arange(KW_ATT)[None, :]
    dist = qi - kj + 2 * TQ_ATT
    qc = qi // CHUNK
    kc = kj // CHUNK
    in_band = (kc >= qc) & (kc <= qc + LEFT_CHUNKS)
    bias = rel_table[:, jnp.clip(dist, REL_MIN, REL_MAX) - REL_MIN].astype(F32)
    return jnp.where(in_band[None], bias, NEG_INF)


def _memkv_kernel(mem_ref, g_ref, w_ref, kg_ref, g128_ref, mk_ref, mv_ref):
    h = _rms(mem_ref[...], g_ref[0]).astype(BF16)
    kv = _dot(h, w_ref[0])
    mk = kv[:, :MEM_WIDTH]
    ms = _group_dot(mk * mk, g128_ref[...])
    mk_ref[0] = (mk * lax.rsqrt(ms + RMS_EPS) * kg_ref[0]).astype(BF16)
    mv_ref[0] = kv[:, MEM_WIDTH:].astype(BF16)


def _memkv(mem, norm_mem, w_kv, k_gain, g128):
    m, d = mem.shape
    depth = w_kv.shape[0]
    return pl.pallas_call(
        _memkv_kernel,
        grid=(depth,),
        in_specs=[_const_spec((m, d)),
                  pl.BlockSpec((1, 1, d), lambda l: (l, 0, 0)),
                  pl.BlockSpec((1, d, 2 * MEM_WIDTH), lambda l: (l, 0, 0)),
                  pl.BlockSpec((1, 1, MEM_WIDTH), lambda l: (l, 0, 0)),
                  _const_spec(g128.shape)],
        out_specs=[pl.BlockSpec((1, m, MEM_WIDTH), lambda l: (l, 0, 0)),
                   pl.BlockSpec((1, m, MEM_WIDTH), lambda l: (l, 0, 0))],
        out_shape=[jax.ShapeDtypeStruct((depth, m, MEM_WIDTH), BF16),
                   jax.ShapeDtypeStruct((depth, m, MEM_WIDTH), BF16)],
        compiler_params=_params(1),
        name="mem_kv",
    )(mem, norm_mem.reshape(depth, 1, d), w_kv, k_gain.reshape(depth, 1, MEM_WIDTH), g128)


def _memattn_kernel(q_ref, k_ref, v_ref, o_ref):
    for h in range(MEM_HEADS):
        cols = slice(h * MEM_HEAD_DIM, (h + 1) * MEM_HEAD_DIM)
        s = _dot_nt(q_ref[:, cols], k_ref[:, cols])
        e = jnp.exp(s - jnp.max(s, axis=-1, keepdims=True))
        denom = jnp.sum(e, axis=-1, keepdims=True)
        o_ref[:, cols] = (_dot(e.astype(BF16), v_ref[:, cols]) / denom).astype(BF16)


def _memattn(mq, mk, mv, seq):
    t = mq.shape[0]
    tm = TM_MEM
    ns = seq // tm
    m = mk.shape[0] // (t // seq)
    kv_spec = pl.BlockSpec((m, MEM_WIDTH), lambda b, i: (b, 0))
    return pl.pallas_call(
        _memattn_kernel,
        grid=(t // seq, ns),
        in_specs=[pl.BlockSpec((tm, MEM_WIDTH), lambda b, i: (b * ns + i, 0)), kv_spec, kv_spec],
        out_specs=pl.BlockSpec((tm, MEM_WIDTH), lambda b, i: (b * ns + i, 0)),
        out_shape=jax.ShapeDtypeStruct((t, MEM_WIDTH), BF16),
        compiler_params=_params(2),
        name="mem_att",
    )(mq, mk, mv)


def _merge_kernel(x_ref, ng_ref, yr_ref, ya_ref, ym_ref, wg_ref, bg_ref, wr_ref, wa_ref, wm_ref,
                  wo_ref, o_ref):
    x = x_ref[...]
    d = x.shape[1]
    h = _rms(x, ng_ref[...]).astype(BF16)
    gates = jax.nn.sigmoid(_dot(h, wg_ref[...]) + bg_ref[...])
    merged = (gates[:, 0:d] * _dot(yr_ref[...], wr_ref[...])
              + gates[:, d:2 * d] * _dot(ya_ref[...], wa_ref[...])
              + gates[:, 2 * d:3 * d] * _dot(ym_ref[...], wm_ref[...]))
    o_ref[...] = x + _dot(merged.astype(BF16), wo_ref[...])


def _merge(x, gain, y_rwkv, y_att, y_mem, w_gate, b_gate, w_r, w_a, w_m, w_out):
    t, d = x.shape
    tm = TM_MERGE
    tok = lambda width: pl.BlockSpec((tm, width), lambda i: (i, 0))
    return pl.pallas_call(
        _merge_kernel,
        grid=(t // tm,),
        in_specs=[tok(d), _const_spec((1, d)), tok(RWKV_WIDTH), tok(ATT_WIDTH), tok(MEM_WIDTH),
                  _const_spec(w_gate.shape), _const_spec((1, 3 * d)),
                  _const_spec(w_r.shape), _const_spec(w_a.shape), _const_spec(w_m.shape),
                  _const_spec(w_out.shape)],
        out_specs=tok(d),
        out_shape=jax.ShapeDtypeStruct((t, d), F32),
        compiler_params=_params(1),
        name="merge",
    )(x, gain.reshape(1, d), y_rwkv, y_att, y_mem, w_gate, b_gate.reshape(1, 3 * d), w_r, w_a, w_m,
      w_out)


def _block_diag_const(width, group, value):
    idx = jnp.arange(width) // group
    return jnp.where(idx[:, None] == idx[None, :], value, 0.0).astype(BF16)


def _chunk_tri(tt):
    i = jnp.arange(tt)
    same_chunk = (i[:, None] // CHUNK) == (i[None, :] // CHUNK)
    return (same_chunk & (i[None, :] <= i[:, None])).astype(BF16)


def kernel(x, mem, norm_ffn1, ffn1_w_in, ffn1_w_out, norm_mix, w_in, shift_mu, decay_w0, decay_lora_b, iclr_a0, iclr_lora_b, gate_lora_b, rwkv_k_k, rwkv_k_a, rwkv_r_k, rwkv_gn_g, rwkv_gn_b, vres_v0, vres_lora_a, vres_lora_b, att_q_norm, att_k_norm, att_rel_bias, norm_mem, mem_w_kv, mem_q_norm, mem_k_norm, w_branch_rwkv, w_branch_att, w_branch_mem, w_gate, b_gate, w_out, norm_ffn2, ffn2_w_in, ffn2_w_out):
    bsz, seq, d = x.shape
    t = bsz * seq
    w_ = RWKV_WIDTH
    depth = w_in.shape[0]
    xf = x.reshape(t, d)

    g64 = _block_diag_const(w_, RWKV_HEAD_DIM, 1.0 / RWKV_HEAD_DIM)
    ones64 = _block_diag_const(w_, RWKV_HEAD_DIM, 1.0)
    g64_qk = _block_diag_const(2 * ATT_WIDTH, ATT_HEAD_DIM, 1.0 / ATT_HEAD_DIM)
    g128 = _block_diag_const(MEM_WIDTH, MEM_HEAD_DIM, 1.0 / MEM_HEAD_DIM)
    tri = _chunk_tri(TT_RWKV)

    mk_all, mv_all = _memkv(mem.reshape(bsz * mem.shape[1], d), norm_mem, mem_w_kv.astype(BF16),
                            jnp.tile(mem_k_norm, (1, MEM_HEADS)), g128)

    v_first = None
    for l in range(depth):
        xf = _ffn(xf, norm_ffn1[l], ffn1_w_in[l].astype(BF16), ffn1_w_out[l].astype(BF16))

        use_vres = l > 0
        w_l = w_in[l]
        vres_a = vres_lora_a[l - 1] if use_vres else jnp.zeros((d, VRES_RANK), F32)
        w1 = jnp.concatenate([w_l[:, :RWKV_IN], vres_a, jnp.zeros((d, LANES - VRES_RANK), F32)],
                             axis=1).astype(BF16)
        w2 = w_l[:, RWKV_IN:RWKV_IN + ATT_IN].astype(BF16)
        w3 = w_l[:, RWKV_IN + ATT_IN:].astype(BF16)
        qk_gain = jnp.concatenate([jnp.tile(att_q_norm[l], ATT_HEADS) * (ATT_HEAD_DIM ** -0.5),
                                   jnp.tile(att_k_norm[l], ATT_HEADS)]).reshape(1, 2 * ATT_WIDTH)
        mq_gain = (jnp.tile(mem_q_norm[l], MEM_HEADS) * (MEM_HEAD_DIM ** -0.5)).reshape(1, MEM_WIDTH)
        p1, qkv, mq = _proj(xf, norm_mix[l], w1, w2, w3, qk_gain, mq_gain, g64_qk, g128)

        zeros_r = jnp.zeros((DECAY_RANK, w_), F32)
        prm = dict(
            mu=jnp.concatenate([shift_mu[l], jnp.zeros((LANES,), F32)]).reshape(1, RWKV_EXT),
            w0=decay_w0[l].reshape(1, w_), a0=iclr_a0[l].reshape(1, w_),
            lora=jnp.concatenate([jnp.concatenate([decay_lora_b[l], zeros_r], axis=1),
                                  jnp.concatenate([zeros_r, iclr_lora_b[l]], axis=1)], axis=0).astype(BF16),
            gate_b=gate_lora_b[l].astype(BF16),
            k_k=rwkv_k_k[l].reshape(1, w_), k_a=rwkv_k_a[l].reshape(1, w_),
            r_k=rwkv_r_k[l].reshape(1, w_), gn_g=rwkv_gn_g[l].reshape(1, w_),
            gn_b=rwkv_gn_b[l].reshape(1, w_), g64=g64, ones64=ones64, tri=tri)
        if use_vres:
            prm["v0"] = vres_v0[l - 1].reshape(1, w_)
            prm["vres_b"] = jnp.concatenate(
                [vres_lora_b[l - 1], jnp.zeros((LANES - VRES_RANK, w_), F32)], axis=0).astype(BF16)
        y_rwkv, v_first = _rwkv(p1, v_first, seq, prm, use_vres)

        y_att = _att(qkv, _att_bias(att_rel_bias[l]), seq)
        y_mem = _memattn(mq, mk_all[l], mv_all[l], seq)

        xf = _merge(xf, norm_mix[l], y_rwkv, y_att, y_mem, w_gate[l].astype(BF16), b_gate[l],
                    w_branch_rwkv[l].astype(BF16), w_branch_att[l].astype(BF16),
                    w_branch_mem[l].astype(BF16), w_out[l].astype(BF16))
        xf = _ffn(xf, norm_ffn2[l], ffn2_w_in[l].astype(BF16), ffn2_w_out[l].astype(BF16))
    return xf.reshape(bsz, seq, d)
```

```python
import functools

import jax
import jax.numpy as jnp
from jax import lax
from jax.experimental import pallas as pl
from jax.experimental.pallas import tpu as pltpu

F32 = jnp.float32
BF16 = jnp.bfloat16

V7X_VMEM_LIMIT_BYTES = 56 * 1024 * 1024

D_MODEL = 1024
DEPTH = 4
CHUNK = 64
RWKV_HEADS = 8
RWKV_HEAD_DIM = 64
RWKV_WIDTH = RWKV_HEADS * RWKV_HEAD_DIM
DECAY_RANK = 64
ICLR_RANK = 64
GATE_RANK = 128
VRES_RANK = 32
ATT_HEADS = 8
ATT_HEAD_DIM = 64
ATT_WIDTH = ATT_HEADS * ATT_HEAD_DIM
LEFT_CHUNKS = 8
REL_MIN = -(CHUNK - 1)
REL_MAX = 128
N_REL = REL_MAX - REL_MIN + 1
MEM_HEADS = 4
MEM_HEAD_DIM = 128
MEM_WIDTH = MEM_HEADS * MEM_HEAD_DIM
D_FF = 2816
RMS_EPS = 1e-6
GN_EPS = 64e-5
L2_EPS = 1e-12
NEG_INF = -1e30

RWKV_IN = 3 * RWKV_WIDTH + DECAY_RANK + ICLR_RANK + GATE_RANK
LANES = 128
MXU_DIM = 256
RWKV_EXT = RWKV_IN + LANES
ATT_IN = 3 * ATT_WIDTH
HEAD_PAIRS = RWKV_WIDTH // LANES

TM_FFN = 512
TM_PROJ = 512
TM_MERGE = 512
TT_RWKV = 256
LOCAL_UNITS = 4
TQ_ATT = 256
KW_ATT = 3 * TQ_ATT
TM_MEM = 512


def _dot(a, b):
    return jnp.dot(a, b, preferred_element_type=F32)


def _dot_nt(a, b):
    return lax.dot_general(a, b, (((1,), (1,)), ((), ())), preferred_element_type=F32)


def _rms(x, gain):
    return x * lax.rsqrt(jnp.mean(x * x, axis=-1, keepdims=True) + RMS_EPS) * gain


def _group_sum(z, g):
    zb = z.astype(BF16)
    return jnp.concatenate([_dot(zb[:, i:i + MXU_DIM], g) for i in range(0, z.shape[1], MXU_DIM)],
                           axis=1)


def _const_spec(shape):
    nd = len(shape)
    return pl.BlockSpec(shape, lambda *_: (0,) * nd, pipeline_mode=pl.Buffered(1))


def _params(n_axes):
    return pltpu.CompilerParams(dimension_semantics=("arbitrary",) * n_axes,
                                vmem_limit_bytes=V7X_VMEM_LIMIT_BYTES)


def _ffn_kernel(x_ref, g_ref, win_ref, wout_ref, o_ref):
    x = x_ref[...]
    h = _rms(x, g_ref[...]).astype(BF16)
    gu = _dot(h, win_ref[...])
    gate = gu[:, :D_FF]
    up = gu[:, D_FF:]
    act = (gate * jax.nn.sigmoid(gate) * up).astype(BF16)
    o_ref[...] = x + 0.5 * _dot(act, wout_ref[...])


def _ffn(x, gain, w_in, w_out):
    t, d = x.shape
    return pl.pallas_call(
        _ffn_kernel,
        grid=(t // TM_FFN,),
        in_specs=[pl.BlockSpec((TM_FFN, d), lambda i: (i, 0)),
                  _const_spec((1, d)),
                  _const_spec(w_in.shape),
                  _const_spec(w_out.shape)],
        out_specs=pl.BlockSpec((TM_FFN, d), lambda i: (i, 0)),
        out_shape=jax.ShapeDtypeStruct((t, d), F32),
        compiler_params=_params(1),
        name="ffn",
    )(x, gain.reshape(1, d), w_in, w_out)


def _proj_kernel(x_ref, g_ref, w1_ref, w2_ref, w3_ref, qkg_ref, mqg_ref, g64_ref, g128_ref,
                 p1_ref, qkv_ref, mq_ref):
    h = _rms(x_ref[...], g_ref[...]).astype(BF16)
    p1_ref[...] = _dot(h, w1_ref[...])
    qkv = _dot(h, w2_ref[...])
    qk = qkv[:, :2 * ATT_WIDTH]
    ms = _group_sum(qk * qk, g64_ref[...])
    qkv_ref[:, :2 * ATT_WIDTH] = (qk * lax.rsqrt(ms + RMS_EPS) * qkg_ref[...]).astype(BF16)
    qkv_ref[:, 2 * ATT_WIDTH:] = qkv[:, 2 * ATT_WIDTH:].astype(BF16)
    mq = _dot(h, w3_ref[...])
    ms = _group_sum(mq * mq, g128_ref[...])
    mq_ref[...] = (mq * lax.rsqrt(ms + RMS_EPS) * mqg_ref[...]).astype(BF16)


def _proj(x, gain, w1, w2, w3, qk_gain, mq_gain, g64, g128):
    t, d = x.shape
    tm = TM_PROJ
    return pl.pallas_call(
        _proj_kernel,
        grid=(t // tm,),
        in_specs=[pl.BlockSpec((tm, d), lambda i: (i, 0)),
                  _const_spec((1, d)),
                  _const_spec(w1.shape), _const_spec(w2.shape), _const_spec(w3.shape),
                  _const_spec(qk_gain.shape), _const_spec(mq_gain.shape),
                  _const_spec(g64.shape), _const_spec(g128.shape)],
        out_specs=[pl.BlockSpec((tm, RWKV_EXT), lambda i: (i, 0)),
                   pl.BlockSpec((tm, ATT_IN), lambda i: (i, 0)),
                   pl.BlockSpec((tm, MEM_WIDTH), lambda i: (i, 0))],
        out_shape=[jax.ShapeDtypeStruct((t, RWKV_EXT), F32),
                   jax.ShapeDtypeStruct((t, ATT_IN), BF16),
                   jax.ShapeDtypeStruct((t, MEM_WIDTH), BF16)],
        compiler_params=_params(1),
        name="proj",
    )(x, gain.reshape(1, d), w1, w2, w3, qk_gain, mq_gain, g64, g128)


def _pair_blockdiag(x, lane_lo):
    zero = jnp.zeros((), x.dtype)
    return jnp.concatenate([jnp.where(lane_lo, x, zero), jnp.where(lane_lo, zero, x)], axis=0)


def _rwkv_kernel(*refs, use_vres, n_batch):
    if use_vres:
        (p_ref, vf_ref, mu_ref, w0_ref, a0_ref, lora_ref, gateb_ref, kk_ref, ka_ref, rk_ref,
         gng_ref, gnb_ref, v0_ref, vresb_ref, g64_ref, ones64_ref, tri_ref,
         y_ref,
         carry_ref, state_ref, rt_ref, at_ref, bt_ref, kt_ref, v_ref, bkt_ref, wcol_ref,
         tinv_ref, arb_ref, rhsl_ref, yl_ref, ys_ref) = refs
        vfo_ref = None
    else:
        (p_ref, mu_ref, w0_ref, a0_ref, lora_ref, gateb_ref, kk_ref, ka_ref, rk_ref,
         gng_ref, gnb_ref, g64_ref, ones64_ref, tri_ref,
         y_ref, vfo_ref,
         carry_ref, state_ref, rt_ref, at_ref, bt_ref, kt_ref, v_ref, bkt_ref, wcol_ref,
         tinv_ref, arb_ref, rhsl_ref, yl_ref, ys_ref) = refs
    tt = p_ref.shape[1]
    rows_all = n_batch * tt
    n_chunks = tt // CHUNK
    n_units = n_batch * n_chunks
    w_ = RWKV_WIDTH

    @pl.when(pl.program_id(0) == 0)
    def _():
        carry_ref[...] = jnp.zeros_like(carry_ref)
        state_ref[...] = jnp.zeros_like(state_ref)

    p = p_ref[...].reshape(rows_all, RWKV_EXT)
    row = lax.broadcasted_iota(jnp.int32, p.shape, 0)
    prev = pltpu.roll(p, 1, axis=0)
    for b in range(n_batch):
        prev = jnp.where(row == b * tt, carry_ref[b:b + 1, :], prev)
    for b in range(n_batch):
        carry_ref[b:b + 1, :] = p[(b + 1) * tt - 1:(b + 1) * tt, :]
    xs = p + (prev - p) * mu_ref[...]

    r = xs[:, 0:w_]
    k = xs[:, w_:2 * w_]
    v = xs[:, 2 * w_:3 * w_]
    lw = xs[:, 3 * w_:3 * w_ + LANES]
    xg = xs[:, 3 * w_ + LANES:3 * w_ + 2 * LANES]

    lane = lax.broadcasted_iota(jnp.int32, lw.shape, 1)
    z = jnp.where(lane < DECAY_RANK, jnp.tanh(lw), lw).astype(BF16)
    lo = _dot(z, lora_ref[...])
    logw = -jnp.exp(F32(-0.5)) * jax.nn.sigmoid(w0_ref[...] + lo[:, :w_])
    a = jax.nn.sigmoid(a0_ref[...] + lo[:, w_:])
    g = _dot(jax.nn.sigmoid(xg).astype(BF16), gateb_ref[...])

    if use_vres:
        hv = xs[:, RWKV_IN:RWKV_EXT].astype(BF16)
        mix = jax.nn.sigmoid(v0_ref[...] + _dot(hv, vresb_ref[...]))
        v = v + (vf_ref[...].reshape(rows_all, w_) - v) * mix
    else:
        vfo_ref[...] = v.reshape(n_batch, tt, w_)

    kk = k * kk_ref[...]
    ss = _group_sum(kk * kk, ones64_ref[...])
    kk = kk * lax.rsqrt(jnp.maximum(ss, L2_EPS * L2_EPS))
    k2 = k * (1.0 + (a - 1.0) * ka_ref[...])
    bvec = kk * a

    tri = tri_ref[...]
    hi = logw.astype(BF16)
    mid = (logw - hi.astype(F32)).astype(BF16)
    cum = jnp.concatenate([_dot(tri, hi[i:i + MXU_DIM]) + _dot(tri, mid[i:i + MXU_DIM])
                           for i in range(0, rows_all, MXU_DIM)], axis=0)
    wcum = jnp.exp(cum)
    inv_w = jnp.exp(-cum)
    bt = bvec * inv_w
    kt = k2 * inv_w
    rt_ref[...] = (r * wcum).astype(BF16)
    at_ref[...] = (-kk * jnp.exp(cum - logw)).astype(BF16)
    bt_ref[...] = bt.astype(BF16)
    kt_ref[...] = kt.astype(BF16)
    v_ref[...] = v.astype(BF16)
    for u in range(n_units):
        lo_row, hi_row = u * CHUNK, (u + 1) * CHUNK
        bkt_ref[u] = jnp.concatenate([bt[lo_row:hi_row], kt[lo_row:hi_row]], axis=0).T.astype(BF16)
        wcol_ref[u] = jnp.broadcast_to(wcum[hi_row - 1:hi_row, :], (LANES, w_)).T

    li = lax.broadcasted_iota(jnp.int32, (CHUNK, LANES), 0)
    mi = lax.broadcasted_iota(jnp.int32, (CHUNK, LANES), 1)
    lane_lo = mi < RWKV_HEAD_DIM
    mloc = jnp.where(lane_lo, mi, mi - RWKV_HEAD_DIM)
    strict = mloc < li
    incl = mloc <= li
    eye2 = (mloc == li).astype(F32)
    bi = lax.broadcasted_iota(jnp.int32, (LANES, LANES), 0)
    bj = lax.broadcasted_iota(jnp.int32, (LANES, LANES), 1)
    diag_blocks = (bi < RWKV_HEAD_DIM) == (bj < RWKV_HEAD_DIM)

    def tile_of(unit, pair):
        return (pl.ds(pl.multiple_of(unit * CHUNK, CHUNK), CHUNK), slice(pair * LANES, (pair + 1) * LANES))

    def local_body(j, carry):
        tiles = [tile_of(LOCAL_UNITS * j + du, pr) for du in range(LOCAL_UNITS) for pr in range(HEAD_PAIRS)]
        pw, tinv = [], []
        for rows, cols in tiles:
            at2, rt2, bt2, kt2, v2 = (ref[rows, cols] for ref in (at_ref, rt_ref, bt_ref, kt_ref, v_ref))
            gram = _dot_nt(jnp.concatenate([at2, rt2], axis=0),
                           jnp.concatenate([_pair_blockdiag(bt2, lane_lo), _pair_blockdiag(kt2, lane_lo)],
                                           axis=0))
            a_ab = jnp.where(strict, gram[:CHUNK, :LANES], 0.0)
            a_ak = jnp.where(strict, gram[:CHUNK, LANES:], 0.0)
            a_rb = jnp.where(incl, gram[CHUNK:, :LANES], 0.0)
            a_rk = jnp.where(incl, gram[CHUNK:, LANES:], 0.0)
            loc = _dot(jnp.concatenate([a_ak, a_rk], axis=0).astype(BF16), _pair_blockdiag(v2, lane_lo))
            rhsl_ref[rows, cols] = loc[:CHUNK]
            yl_ref[rows, cols] = loc[CHUNK:]
            arb_ref[rows, cols] = a_rb.astype(BF16)
            pw.append(a_ab)
            tinv.append(eye2 + a_ab)
        pwb = [x.astype(BF16) for x in pw]
        pw = [_dot(x, _pair_blockdiag(x, lane_lo)) for x in pwb]
        for level in range(5):
            pwb = [x.astype(BF16) for x in pw]
            if level < 4:
                both = [_dot(jnp.concatenate([x, tv.astype(BF16)], axis=0), _pair_blockdiag(x, lane_lo))
                        for x, tv in zip(pwb, tinv)]
                pw = [m[:CHUNK] for m in both]
                tinv = [tv + m[CHUNK:] for tv, m in zip(tinv, both)]
            else:
                tinv = [tv + _dot(tv.astype(BF16), _pair_blockdiag(x, lane_lo))
                        for x, tv in zip(pwb, tinv)]
        for (rows, cols), tv in zip(tiles, tinv):
            tinv_ref[rows, cols] = tv.astype(BF16)
        return carry

    lax.fori_loop(0, n_units // LOCAL_UNITS, local_body, 0)

    def seq_body(c, carry):
        units = [(b * n_chunks + c, b * HEAD_PAIRS + pr, pr)
                 for b in range(n_batch) for pr in range(HEAD_PAIRS)]
        tiles = [tile_of(u, pr) for u, _, pr in units]
        st = [state_ref[si] for _, si, _ in units]
        m1 = [_dot(jnp.concatenate([at_ref[rows, cols], rt_ref[rows, cols]], axis=0), s.astype(BF16))
              for (rows, cols), s in zip(tiles, st)]
        rhs = [m[:CHUNK] + rhsl_ref[rows, cols] for m, (rows, cols) in zip(m1, tiles)]
        ub = [_dot(tinv_ref[rows, cols], _pair_blockdiag(x.astype(BF16), lane_lo)).astype(BF16)
              for x, (rows, cols) in zip(rhs, tiles)]
        for (u, si, pr), (rows, cols), s, x in zip(units, tiles, st, ub):
            chans = slice(pr * LANES, (pr + 1) * LANES)
            upd = _dot(bkt_ref[u, chans, :], jnp.concatenate([x, v_ref[rows, cols]], axis=0))
            state_ref[si] = (s + jnp.where(diag_blocks, upd, 0.0)) * wcol_ref[u, chans, :]
        for m, (rows, cols), x in zip(m1, tiles, ub):
            ys_ref[rows, cols] = (m[CHUNK:] + _dot(arb_ref[rows, cols], _pair_blockdiag(x, lane_lo))
                                  + yl_ref[rows, cols])
        return carry

    lax.fori_loop(0, n_chunks, seq_body, 0)

    y = ys_ref[...]
    mean = _group_sum(y, g64_ref[...])
    yc = y - mean
    var = _group_sum(yc * yc, g64_ref[...])
    yn = yc * lax.rsqrt(var + GN_EPS) * gng_ref[...] + gnb_ref[...]
    bonus = _group_sum(r * k2 * rk_ref[...], ones64_ref[...]) * v
    y_ref[...] = ((yn + bonus) * g).astype(BF16).reshape(n_batch, tt, w_)


def _rwkv(p1, v_first, prm, use_vres):
    n_batch, seq, _ = p1.shape
    tt = TT_RWKV
    rows_all = n_batch * tt
    n_units = rows_all // CHUNK
    w_ = RWKV_WIDTH
    tok = lambda width: pl.BlockSpec((n_batch, tt, width), lambda i: (0, i, 0))
    vec = _const_spec((1, w_))
    in_specs = [tok(RWKV_EXT)]
    args = [p1]
    if use_vres:
        in_specs.append(tok(w_))
        args.append(v_first)
    in_specs += [_const_spec((1, RWKV_EXT)), vec, vec, _const_spec((LANES, 2 * w_)),
                 _const_spec((GATE_RANK, w_)), vec, vec, vec, vec, vec]
    args += [prm["mu"], prm["w0"], prm["a0"], prm["lora"], prm["gate_b"], prm["k_k"], prm["k_a"],
             prm["r_k"], prm["gn_g"], prm["gn_b"]]
    if use_vres:
        in_specs += [vec, _const_spec((LANES, w_))]
        args += [prm["v0"], prm["vres_b"]]
    in_specs += [_const_spec((MXU_DIM, MXU_DIM))] * 3
    args += [prm["g64"], prm["ones64"], prm["tri"]]
    out_specs = [tok(w_)]
    out_shape = [jax.ShapeDtypeStruct((n_batch, seq, w_), BF16)]
    if not use_vres:
        out_specs.append(tok(w_))
        out_shape.append(jax.ShapeDtypeStruct((n_batch, seq, w_), F32))
    act_bf16 = pltpu.VMEM((rows_all, w_), BF16)
    act_f32 = pltpu.VMEM((rows_all, w_), F32)
    scratch = [pltpu.VMEM((8, RWKV_EXT), F32),
               pltpu.VMEM((n_batch * HEAD_PAIRS, LANES, LANES), F32),
               act_bf16, act_bf16, act_bf16, act_bf16, act_bf16,
               pltpu.VMEM((n_units, w_, LANES), BF16),
               pltpu.VMEM((n_units, w_, LANES), F32),
               act_bf16, act_bf16, act_f32, act_f32, act_f32]
    out = pl.pallas_call(
        functools.partial(_rwkv_kernel, use_vres=use_vres, n_batch=n_batch),
        grid=(seq // tt,),
        in_specs=in_specs,
        out_specs=out_specs,
        out_shape=out_shape,
        scratch_shapes=scratch,
        compiler_params=_params(1),
        name="rwkv",
    )(*args)
    if use_vres:
        return out[0], v_first
    return out[0], out[1]


def _att_kernel(q_ref, k0_ref, k1_ref, k2_ref, v0_ref, v1_ref, v2_ref, bias_ref, o_ref):
    i = pl.program_id(1)
    tq = q_ref.shape[0]
    kw = 3 * tq
    col = lax.broadcasted_iota(jnp.int32, (tq, kw), 1)
    valid = col >= (2 - i) * tq
    lane = lax.broadcasted_iota(jnp.int32, (1, LANES), 1)
    zero = jnp.zeros((), BF16)
    for hp in range(ATT_HEADS // 2):
        cols = slice(hp * LANES, (hp + 1) * LANES)
        q2 = q_ref[:, cols]
        k2 = jnp.concatenate([k0_ref[:, cols], k1_ref[:, cols], k2_ref[:, cols]], axis=0)
        v2 = jnp.concatenate([v0_ref[:, cols], v1_ref[:, cols], v2_ref[:, cols]], axis=0)
        out = jnp.zeros((tq, LANES), F32)
        for sub in range(2):
            head_lanes = (lane >= sub * ATT_HEAD_DIM) & (lane < (sub + 1) * ATT_HEAD_DIM)
            s = _dot_nt(jnp.where(head_lanes, q2, zero), k2) + bias_ref[2 * hp + sub]
            s = jnp.where(valid, s, NEG_INF)
            e = jnp.exp(s - jnp.max(s, axis=-1, keepdims=True))
            denom = jnp.sum(e, axis=-1, keepdims=True)
            pv = _dot(e.astype(BF16), jnp.where(head_lanes, v2, zero))
            out = out + pv / denom
        o_ref[:, cols] = out.astype(BF16)


def _att(qkv, bias, seq):
    t = qkv.shape[0]
    tq = TQ_ATT
    nq = seq // tq
    blk = lambda col, back: pl.BlockSpec(
        (tq, ATT_WIDTH), lambda b, i: (b * nq + jnp.maximum(i - back, 0), col))
    return pl.pallas_call(
        _att_kernel,
        grid=(t // seq, nq),
        in_specs=[blk(0, 0), blk(1, 2), blk(1, 1), blk(1, 0), blk(2, 2), blk(2, 1), blk(2, 0),
                  _const_spec(bias.shape)],
        out_specs=pl.BlockSpec((tq, ATT_WIDTH), lambda b, i: (b * nq + i, 0)),
        out_shape=jax.ShapeDtypeStruct((t, ATT_WIDTH), BF16),
        compiler_params=_params(2),
        name="band_att",
    )(qkv, qkv, qkv, qkv, qkv, qkv, qkv, bias)


def _att_bias(rel_table):
    heads = rel_table.shape[0]
    far = rel_table[:, N_REL - 1:]
    near = rel_table[:, :1]
    n_far = 2 * TQ_ATT - REL_MAX
    period = KW_ATT + TQ_ATT + 1
    base = jnp.concatenate([
        jnp.broadcast_to(far, (heads, n_far)),
        rel_table[:, ::-1],
        jnp.broadcast_to(near, (heads, KW_ATT - n_far - N_REL)),
        jnp.broadcast_to(far, (heads, period - KW_ATT)),
    ], axis=1)
    rows = jnp.tile(base, (1, TQ_ATT))[:, :TQ_ATT * (period - 1)].reshape(heads, TQ_ATT, period - 1)
    bias = rows[:, :, :KW_ATT].astype(F32)
    qc = jnp.arange(TQ_ATT)[:, None] // CHUNK
    kc = jnp.arange(KW_ATT)[None, :] // CHUNK
    in_band = (kc >= qc) & (kc <= qc + LEFT_CHUNKS)
    return jnp.where(in_band[None], bias, NEG_INF)


def _memkv_kernel(mem_ref, g_ref, w_ref, kg_ref, g128_ref, mk_ref, mv_ref):
    h = _rms(mem_ref[...], g_ref[0]).astype(BF16)
    kv = _dot(h, w_ref[0])
    mk = kv[:, :MEM_WIDTH]
    ms = _group_sum(mk * mk, g128_ref[...])
    mk_ref[0] = (mk * lax.rsqrt(ms + RMS_EPS) * kg_ref[0]).astype(BF16)
    mv_ref[0] = kv[:, MEM_WIDTH:].astype(BF16)


def _memkv(mem, norm_mem, w_kv, k_gain, g128):
    m, d = mem.shape
    depth = w_kv.shape[0]
    return pl.pallas_call(
        _memkv_kernel,
        grid=(depth,),
        in_specs=[_const_spec((m, d)),
                  pl.BlockSpec((1, 1, d), lambda l: (l, 0, 0)),
                  pl.BlockSpec((1, d, 2 * MEM_WIDTH), lambda l: (l, 0, 0)),
                  pl.BlockSpec((1, 1, MEM_WIDTH), lambda l: (l, 0, 0)),
                  _const_spec(g128.shape)],
        out_specs=[pl.BlockSpec((1, m, MEM_WIDTH), lambda l: (l, 0, 0)),
                   pl.BlockSpec((1, m, MEM_WIDTH), lambda l: (l, 0, 0))],
        out_shape=[jax.ShapeDtypeStruct((depth, m, MEM_WIDTH), BF16),
                   jax.ShapeDtypeStruct((depth, m, MEM_WIDTH), BF16)],
        compiler_params=_params(1),
        name="mem_kv",
    )(mem, norm_mem.reshape(depth, 1, d), w_kv, k_gain.reshape(depth, 1, MEM_WIDTH), g128)


def _memattn_kernel(q_ref, k_ref, v_ref, o_ref):
    for h in range(MEM_HEADS):
        cols = slice(h * MEM_HEAD_DIM, (h + 1) * MEM_HEAD_DIM)
        s = _dot_nt(q_ref[:, cols], k_ref[:, cols])
        e = jnp.exp(s - jnp.max(s, axis=-1, keepdims=True))
        denom = jnp.sum(e, axis=-1, keepdims=True)
        o_ref[:, cols] = (_dot(e.astype(BF16), v_ref[:, cols]) / denom).astype(BF16)


def _memattn(mq, mk, mv, seq):
    t = mq.shape[0]
    tm = TM_MEM
    ns = seq // tm
    m = mk.shape[0] // (t // seq)
    kv_spec = pl.BlockSpec((m, MEM_WIDTH), lambda b, i: (b, 0))
    return pl.pallas_call(
        _memattn_kernel,
        grid=(t // seq, ns),
        in_specs=[pl.BlockSpec((tm, MEM_WIDTH), lambda b, i: (b * ns + i, 0)), kv_spec, kv_spec],
        out_specs=pl.BlockSpec((tm, MEM_WIDTH), lambda b, i: (b * ns + i, 0)),
        out_shape=jax.ShapeDtypeStruct((t, MEM_WIDTH), BF16),
        compiler_params=_params(2),
        name="mem_att",
    )(mq, mk, mv)


def _merge_kernel(x_ref, ng_ref, yr_ref, ya_ref, ym_ref, wg_ref, bg_ref, wr_ref, wa_ref, wm_ref,
                  wo_ref, o_ref):
    x = x_ref[...]
    d = x.shape[1]
    h = _rms(x, ng_ref[...]).astype(BF16)
    gates = jax.nn.sigmoid(_dot(h, wg_ref[...]) + bg_ref[...])
    merged = (gates[:, 0:d] * _dot(yr_ref[...], wr_ref[...])
              + gates[:, d:2 * d] * _dot(ya_ref[...], wa_ref[...])
              + gates[:, 2 * d:3 * d] * _dot(ym_ref[...], wm_ref[...]))
    o_ref[...] = x + _dot(merged.astype(BF16), wo_ref[...])


def _merge(x, gain, y_rwkv, y_att, y_mem, w_gate, b_gate, w_r, w_a, w_m, w_out):
    t, d = x.shape
    tm = TM_MERGE
    tok = lambda width: pl.BlockSpec((tm, width), lambda i: (i, 0))
    return pl.pallas_call(
        _merge_kernel,
        grid=(t // tm,),
        in_specs=[tok(d), _const_spec((1, d)), tok(RWKV_WIDTH), tok(ATT_WIDTH), tok(MEM_WIDTH),
                  _const_spec(w_gate.shape), _const_spec((1, 3 * d)),
                  _const_spec(w_r.shape), _const_spec(w_a.shape), _const_spec(w_m.shape),
                  _const_spec(w_out.shape)],
        out_specs=tok(d),
        out_shape=jax.ShapeDtypeStruct((t, d), F32),
        compiler_params=_params(1),
        name="merge",
    )(x, gain.reshape(1, d), y_rwkv, y_att, y_mem, w_gate, b_gate.reshape(1, 3 * d), w_r, w_a, w_m,
      w_out)


def _block_diag_const(width, group, value):
    idx = jnp.arange(width) // group
    return jnp.where(idx[:, None] == idx[None, :], value, 0.0).astype(BF16)


def _chunk_tri(rows):
    i = jnp.arange(rows)
    same_chunk = (i[:, None] // CHUNK) == (i[None, :] // CHUNK)
    return (same_chunk & (i[None, :] <= i[:, None])).astype(BF16)


def kernel(x, mem, norm_ffn1, ffn1_w_in, ffn1_w_out, norm_mix, w_in, shift_mu, decay_w0, decay_lora_b, iclr_a0, iclr_lora_b, gate_lora_b, rwkv_k_k, rwkv_k_a, rwkv_r_k, rwkv_gn_g, rwkv_gn_b, vres_v0, vres_lora_a, vres_lora_b, att_q_norm, att_k_norm, att_rel_bias, norm_mem, mem_w_kv, mem_q_norm, mem_k_norm, w_branch_rwkv, w_branch_att, w_branch_mem, w_gate, b_gate, w_out, norm_ffn2, ffn2_w_in, ffn2_w_out):
    bsz, seq, d = x.shape
    t = bsz * seq
    w_ = RWKV_WIDTH
    depth = w_in.shape[0]
    xf = x.reshape(t, d)

    g64 = _block_diag_const(MXU_DIM, RWKV_HEAD_DIM, 1.0 / RWKV_HEAD_DIM)
    ones64 = _block_diag_const(MXU_DIM, RWKV_HEAD_DIM, 1.0)
    g128 = _block_diag_const(MXU_DIM, MEM_HEAD_DIM, 1.0 / MEM_HEAD_DIM)
    tri = _chunk_tri(MXU_DIM)

    mk_all, mv_all = _memkv(mem.reshape(bsz * mem.shape[1], d), norm_mem, mem_w_kv.astype(BF16),
                            jnp.tile(mem_k_norm, (1, MEM_HEADS)), g128)

    v_first = None
    for l in range(depth):
        xf = _ffn(xf, norm_ffn1[l], ffn1_w_in[l].astype(BF16), ffn1_w_out[l].astype(BF16))

        use_vres = l > 0
        w_l = w_in[l]
        vres_a = vres_lora_a[l - 1] if use_vres else jnp.zeros((d, VRES_RANK), F32)
        w1 = jnp.concatenate([w_l[:, :RWKV_IN], vres_a, jnp.zeros((d, LANES - VRES_RANK), F32)],
                             axis=1).astype(BF16)
        w2 = w_l[:, RWKV_IN:RWKV_IN + ATT_IN].astype(BF16)
        w3 = w_l[:, RWKV_IN + ATT_IN:].astype(BF16)
        qk_gain = jnp.concatenate([jnp.tile(att_q_norm[l], ATT_HEADS) * (ATT_HEAD_DIM ** -0.5),
                                   jnp.tile(att_k_norm[l], ATT_HEADS)]).reshape(1, 2 * ATT_WIDTH)
        mq_gain = (jnp.tile(mem_q_norm[l], MEM_HEADS) * (MEM_HEAD_DIM ** -0.5)).reshape(1, MEM_WIDTH)
        p1, qkv, mq = _proj(xf, norm_mix[l], w1, w2, w3, qk_gain, mq_gain, g64, g128)

        zeros_r = jnp.zeros((DECAY_RANK, w_), F32)
        prm = dict(
            mu=jnp.concatenate([shift_mu[l], jnp.zeros((LANES,), F32)]).reshape(1, RWKV_EXT),
            w0=decay_w0[l].reshape(1, w_), a0=iclr_a0[l].reshape(1, w_),
            lora=jnp.concatenate([jnp.concatenate([decay_lora_b[l], zeros_r], axis=1),
                                  jnp.concatenate([zeros_r, iclr_lora_b[l]], axis=1)], axis=0).astype(BF16),
            gate_b=gate_lora_b[l].astype(BF16),
            k_k=rwkv_k_k[l].reshape(1, w_), k_a=rwkv_k_a[l].reshape(1, w_),
            r_k=rwkv_r_k[l].reshape(1, w_), gn_g=rwkv_gn_g[l].reshape(1, w_),
            gn_b=rwkv_gn_b[l].reshape(1, w_), g64=g64, ones64=ones64, tri=tri)
        if use_vres:
            prm["v0"] = vres_v0[l - 1].reshape(1, w_)
            prm["vres_b"] = jnp.concatenate(
                [vres_lora_b[l - 1], jnp.zeros((LANES - VRES_RANK, w_), F32)], axis=0).astype(BF16)
        y_rwkv, v_first = _rwkv(p1.reshape(bsz, seq, RWKV_EXT), v_first, prm, use_vres)

        y_att = _att(qkv, _att_bias(att_rel_bias[l]), seq)
        y_mem = _memattn(mq, mk_all[l], mv_all[l], seq)

        xf = _merge(xf, norm_mix[l], y_rwkv.reshape(t, w_), y_att, y_mem, w_gate[l].astype(BF16),
                    b_gate[l], w_branch_rwkv[l].astype(BF16), w_branch_att[l].astype(BF16),
                    w_branch_mem[l].astype(BF16), w_out[l].astype(BF16))
        xf = _ffn(xf, norm_ffn2[l], ffn2_w_in[l].astype(BF16), ffn2_w_out[l].astype(BF16))
    return xf.reshape(bsz, seq, d)
```

```python
import functools

import jax
import jax.numpy as jnp
from jax import lax
from jax.experimental import pallas as pl
from jax.experimental.pallas import tpu as pltpu

F32 = jnp.float32
BF16 = jnp.bfloat16

V7X_VMEM_LIMIT_BYTES = 56 * 1024 * 1024

D_MODEL = 1024
DEPTH = 4
CHUNK = 64
RWKV_HEADS = 8
RWKV_HEAD_DIM = 64
RWKV_WIDTH = RWKV_HEADS * RWKV_HEAD_DIM
DECAY_RANK = 64
ICLR_RANK = 64
GATE_RANK = 128
VRES_RANK = 32
ATT_HEADS = 8
ATT_HEAD_DIM = 64
ATT_WIDTH = ATT_HEADS * ATT_HEAD_DIM
LEFT_CHUNKS = 8
REL_MIN = -(CHUNK - 1)
REL_MAX = 128
N_REL = REL_MAX - REL_MIN + 1
MEM_HEADS = 4
MEM_HEAD_DIM = 128
MEM_WIDTH = MEM_HEADS * MEM_HEAD_DIM
D_FF = 2816
RMS_EPS = 1e-6
GN_EPS = 64e-5
L2_EPS = 1e-12
NEG_INF = -1e30

RWKV_IN = 3 * RWKV_WIDTH + DECAY_RANK + ICLR_RANK + GATE_RANK
LANES = 128
MXU_DIM = 256
RWKV_EXT = RWKV_IN + LANES
ATT_IN = 3 * ATT_WIDTH
HEAD_PAIRS = RWKV_WIDTH // LANES

TM_FFN = 512
TM_PROJ = 512
TM_MERGE = 512
TT_RWKV = 256
TQ_ATT = 256
TH_ATT = 128
KH_ATT = TH_ATT + LEFT_CHUNKS * CHUNK
PAIRS_PER_STAGE = 4
LOG2_E = 1.4426950408889634
TM_MEM = 512


def _dot(a, b):
    return jnp.dot(a, b, preferred_element_type=F32)


def _dot_nt(a, b):
    return lax.dot_general(a, b, (((1,), (1,)), ((), ())), preferred_element_type=F32)


def _rms(x, gain):
    return x * lax.rsqrt(jnp.mean(x * x, axis=-1, keepdims=True) + RMS_EPS) * gain


def _group_sum(z, g):
    zb = z.astype(BF16)
    return jnp.concatenate([_dot(zb[:, i:i + MXU_DIM], g) for i in range(0, z.shape[1], MXU_DIM)],
                           axis=1)


def _const_spec(shape):
    nd = len(shape)
    return pl.BlockSpec(shape, lambda *_: (0,) * nd, pipeline_mode=pl.Buffered(1))


def _layer_spec(stacked_shape, layer):
    nd = len(stacked_shape) - 1
    return pl.BlockSpec((None,) + tuple(stacked_shape[1:]), lambda *_: (layer,) + (0,) * nd,
                        pipeline_mode=pl.Buffered(1))


def _params(n_axes):
    return pltpu.CompilerParams(dimension_semantics=("arbitrary",) * n_axes,
                                vmem_limit_bytes=V7X_VMEM_LIMIT_BYTES)


def _ffn_kernel(x_ref, g_ref, win_ref, wout_ref, o_ref):
    x = x_ref[...]
    h = _rms(x, g_ref[...]).astype(BF16)
    gu = _dot(h, win_ref[...])
    gate = gu[:, :D_FF]
    up = gu[:, D_FF:]
    act = (gate * jax.nn.sigmoid(gate) * up).astype(BF16)
    o_ref[...] = x + 0.5 * _dot(act, wout_ref[...])


def _ffn(x, gain, w_in, w_out, layer):
    t, d = x.shape
    return pl.pallas_call(
        _ffn_kernel,
        grid=(t // TM_FFN,),
        in_specs=[pl.BlockSpec((TM_FFN, d), lambda i: (i, 0)),
                  _const_spec((1, d)),
                  _layer_spec(w_in.shape, layer),
                  _layer_spec(w_out.shape, layer)],
        out_specs=pl.BlockSpec((TM_FFN, d), lambda i: (i, 0)),
        out_shape=jax.ShapeDtypeStruct((t, d), F32),
        compiler_params=_params(1),
        name="ffn",
    )(x, gain.reshape(1, d), w_in, w_out)


def _proj_kernel(x_ref, g_ref, w1_ref, w2_ref, w3_ref, qkg_ref, mqg_ref, g64_ref, g128_ref,
                 p1_ref, qkv_ref, mq_ref):
    h = _rms(x_ref[...], g_ref[...]).astype(BF16)
    p1_ref[...] = _dot(h, w1_ref[...])
    qkv = _dot(h, w2_ref[...])
    qk = qkv[:, :2 * ATT_WIDTH]
    ms = _group_sum(qk * qk, g64_ref[...])
    qkv_ref[:, :2 * ATT_WIDTH] = (qk * lax.rsqrt(ms + RMS_EPS) * qkg_ref[...]).astype(BF16)
    qkv_ref[:, 2 * ATT_WIDTH:] = qkv[:, 2 * ATT_WIDTH:].astype(BF16)
    mq = _dot(h, w3_ref[...])
    ms = _group_sum(mq * mq, g128_ref[...])
    mq_ref[...] = (mq * lax.rsqrt(ms + RMS_EPS) * mqg_ref[...]).astype(BF16)


def _proj(x, gain, w1, w2, w3, qk_gain, mq_gain, g64, g128):
    t, d = x.shape
    tm = TM_PROJ
    return pl.pallas_call(
        _proj_kernel,
        grid=(t // tm,),
        in_specs=[pl.BlockSpec((tm, d), lambda i: (i, 0)),
                  _const_spec((1, d)),
                  _const_spec(w1.shape), _const_spec(w2.shape), _const_spec(w3.shape),
                  _const_spec(qk_gain.shape), _const_spec(mq_gain.shape),
                  _const_spec(g64.shape), _const_spec(g128.shape)],
        out_specs=[pl.BlockSpec((tm, RWKV_EXT), lambda i: (i, 0)),
                   pl.BlockSpec((tm, ATT_IN), lambda i: (i, 0)),
                   pl.BlockSpec((tm, MEM_WIDTH), lambda i: (i, 0))],
        out_shape=[jax.ShapeDtypeStruct((t, RWKV_EXT), F32),
                   jax.ShapeDtypeStruct((t, ATT_IN), BF16),
                   jax.ShapeDtypeStruct((t, MEM_WIDTH), BF16)],
        compiler_params=_params(1),
        name="proj",
    )(x, gain.reshape(1, d), w1, w2, w3, qk_gain, mq_gain, g64, g128)


def _pair_blockdiag(x, lane_lo):
    zero = jnp.zeros((), x.dtype)
    return jnp.concatenate([jnp.where(lane_lo, x, zero), jnp.where(lane_lo, zero, x)], axis=0)


def _rwkv_kernel(*refs, use_vres, n_batch):
    if use_vres:
        (p_ref, vf_ref, mu_ref, w0_ref, a0_ref, lora_ref, gateb_ref, kk_ref, ka_ref, rk_ref,
         gng_ref, gnb_ref, v0_ref, vresb_ref, g64_ref, ones64_ref, tri_ref,
         y_ref,
         carry_ref, state_ref, rt_ref, at_ref, bt_ref, kt_ref, v_ref, bkt_ref, wcol_ref,
         tinv_ref, arb_ref, rhsl_ref, yl_ref, ys_ref) = refs
        vfo_ref = None
    else:
        (p_ref, mu_ref, w0_ref, a0_ref, lora_ref, gateb_ref, kk_ref, ka_ref, rk_ref,
         gng_ref, gnb_ref, g64_ref, ones64_ref, tri_ref,
         y_ref, vfo_ref,
         carry_ref, state_ref, rt_ref, at_ref, bt_ref, kt_ref, v_ref, bkt_ref, wcol_ref,
         tinv_ref, arb_ref, rhsl_ref, yl_ref, ys_ref) = refs
    tt = p_ref.shape[1]
    rows_all = n_batch * tt
    n_chunks = tt // CHUNK
    n_units = n_batch * n_chunks
    w_ = RWKV_WIDTH

    @pl.when(pl.program_id(0) == 0)
    def _():
        carry_ref[...] = jnp.zeros_like(carry_ref)
        state_ref[...] = jnp.zeros_like(state_ref)

    p = p_ref[...].reshape(rows_all, RWKV_EXT)
    row = lax.broadcasted_iota(jnp.int32, p.shape, 0)
    prev = pltpu.roll(p, 1, axis=0)
    for b in range(n_batch):
        prev = jnp.where(row == b * tt, carry_ref[b:b + 1, :], prev)
    for b in range(n_batch):
        carry_ref[b:b + 1, :] = p[(b + 1) * tt - 1:(b + 1) * tt, :]
    xs = p + (prev - p) * mu_ref[...]

    r = xs[:, 0:w_]
    k = xs[:, w_:2 * w_]
    v = xs[:, 2 * w_:3 * w_]
    lw = xs[:, 3 * w_:3 * w_ + LANES]
    xg = xs[:, 3 * w_ + LANES:3 * w_ + 2 * LANES]

    lane = lax.broadcasted_iota(jnp.int32, lw.shape, 1)
    z = jnp.where(lane < DECAY_RANK, jnp.tanh(lw), lw).astype(BF16)
    lo = _dot(z, lora_ref[...])
    logw = -jnp.exp(F32(-0.5)) * jax.nn.sigmoid(w0_ref[...] + lo[:, :w_])
    a = jax.nn.sigmoid(a0_ref[...] + lo[:, w_:])
    g = _dot(jax.nn.sigmoid(xg).astype(BF16), gateb_ref[...])

    if use_vres:
        hv = xs[:, RWKV_IN:RWKV_EXT].astype(BF16)
        mix = jax.nn.sigmoid(v0_ref[...] + _dot(hv, vresb_ref[...]))
        v = v + (vf_ref[...].reshape(rows_all, w_) - v) * mix
    else:
        vfo_ref[...] = v.reshape(n_batch, tt, w_)

    kk = k * kk_ref[...]
    ss = _group_sum(kk * kk, ones64_ref[...])
    kk = kk * lax.rsqrt(jnp.maximum(ss, L2_EPS * L2_EPS))
    k2 = k * (1.0 + (a - 1.0) * ka_ref[...])
    bvec = kk * a

    tri = tri_ref[...]
    hi = logw.astype(BF16)
    mid = (logw - hi.astype(F32)).astype(BF16)
    cum = jnp.concatenate([_dot(tri, hi[i:i + MXU_DIM]) + _dot(tri, mid[i:i + MXU_DIM])
                           for i in range(0, rows_all, MXU_DIM)], axis=0)
    wcum = jnp.exp(cum)
    inv_w = jnp.exp(-cum)
    bt = bvec * inv_w
    kt = k2 * inv_w
    rt_ref[...] = (r * wcum).astype(BF16)
    at_ref[...] = (-kk * jnp.exp(cum - logw)).astype(BF16)
    bt_ref[...] = bt.astype(BF16)
    kt_ref[...] = kt.astype(BF16)
    v_ref[...] = v.astype(BF16)
    for u in range(n_units):
        lo_row, hi_row = u * CHUNK, (u + 1) * CHUNK
        bkt_ref[u] = jnp.concatenate([bt[lo_row:hi_row], kt[lo_row:hi_row]], axis=0).T.astype(BF16)
        wcol_ref[u] = jnp.broadcast_to(wcum[hi_row - 1:hi_row, :], (LANES, w_)).T

    li = lax.broadcasted_iota(jnp.int32, (CHUNK, LANES), 0)
    mi = lax.broadcasted_iota(jnp.int32, (CHUNK, LANES), 1)
    lane_lo = mi < RWKV_HEAD_DIM
    mloc = jnp.where(lane_lo, mi, mi - RWKV_HEAD_DIM)
    strict = mloc < li
    incl = mloc <= li
    eye2 = (mloc == li).astype(F32)
    bi = lax.broadcasted_iota(jnp.int32, (LANES, LANES), 0)
    bj = lax.broadcasted_iota(jnp.int32, (LANES, LANES), 1)
    diag_blocks = (bi < RWKV_HEAD_DIM) == (bj < RWKV_HEAD_DIM)

    def tile_of(unit, pair):
        return (slice(unit * CHUNK, (unit + 1) * CHUNK), slice(pair * LANES, (pair + 1) * LANES))

    tiles = [tile_of(u, pr) for u in range(n_units) for pr in range(HEAD_PAIRS)]
    pw, tinv = [], []
    for rows, cols in tiles:
        at2, rt2, bt2, kt2, v2 = (ref[rows, cols] for ref in (at_ref, rt_ref, bt_ref, kt_ref, v_ref))
        gram = _dot_nt(jnp.concatenate([at2, rt2], axis=0),
                       jnp.concatenate([_pair_blockdiag(bt2, lane_lo), _pair_blockdiag(kt2, lane_lo)],
                                       axis=0))
        a_ab = jnp.where(strict, gram[:CHUNK, :LANES], 0.0)
        a_ak = jnp.where(strict, gram[:CHUNK, LANES:], 0.0)
        a_rb = jnp.where(incl, gram[CHUNK:, :LANES], 0.0)
        a_rk = jnp.where(incl, gram[CHUNK:, LANES:], 0.0)
        loc = _dot(jnp.concatenate([a_ak, a_rk], axis=0).astype(BF16), _pair_blockdiag(v2, lane_lo))
        rhsl_ref[rows, cols] = loc[:CHUNK]
        yl_ref[rows, cols] = loc[CHUNK:]
        arb_ref[rows, cols] = a_rb.astype(BF16)
        pw.append(a_ab)
        tinv.append(eye2 + a_ab)
    pwb = [x.astype(BF16) for x in pw]
    pw = [_dot(x, _pair_blockdiag(x, lane_lo)) for x in pwb]
    for level in range(5):
        pwb = [x.astype(BF16) for x in pw]
        if level < 4:
            both = [_dot(jnp.concatenate([x, tv.astype(BF16)], axis=0), _pair_blockdiag(x, lane_lo))
                    for x, tv in zip(pwb, tinv)]
            pw = [m[:CHUNK] for m in both]
            tinv = [tv + m[CHUNK:] for tv, m in zip(tinv, both)]
        else:
            tinv = [tv + _dot(tv.astype(BF16), _pair_blockdiag(x, lane_lo))
                    for x, tv in zip(pwb, tinv)]
    for (rows, cols), tv in zip(tiles, tinv):
        tinv_ref[rows, cols] = tv.astype(BF16)

    chains = [(b, pr) for b in range(n_batch) for pr in range(HEAD_PAIRS)]
    st = [state_ref[b * HEAD_PAIRS + pr] for b, pr in chains]
    for c in range(n_chunks):
        units = [b * n_chunks + c for b, _ in chains]
        tiles = [tile_of(u, pr) for u, (_, pr) in zip(units, chains)]
        m1 = [_dot(jnp.concatenate([at_ref[rows, cols], rt_ref[rows, cols]], axis=0), s.astype(BF16))
              for (rows, cols), s in zip(tiles, st)]
        rhs = [m[:CHUNK] + rhsl_ref[rows, cols] for m, (rows, cols) in zip(m1, tiles)]
        ub = [_dot(tinv_ref[rows, cols], _pair_blockdiag(x.astype(BF16), lane_lo)).astype(BF16)
              for x, (rows, cols) in zip(rhs, tiles)]
        new_st = []
        for u, (_, pr), (rows, cols), s, x in zip(units, chains, tiles, st, ub):
            chans = slice(pr * LANES, (pr + 1) * LANES)
            upd = _dot(bkt_ref[u, chans, :], jnp.concatenate([x, v_ref[rows, cols]], axis=0))
            new_st.append((s + jnp.where(diag_blocks, upd, 0.0)) * wcol_ref[u, chans, :])
        st = new_st
        for m, (rows, cols), x in zip(m1, tiles, ub):
            ys_ref[rows, cols] = (m[CHUNK:] + _dot(arb_ref[rows, cols], _pair_blockdiag(x, lane_lo))
                                  + yl_ref[rows, cols])
    for (b, pr), s in zip(chains, st):
        state_ref[b * HEAD_PAIRS + pr] = s

    y = ys_ref[...]
    mean = _group_sum(y, g64_ref[...])
    yc = y - mean
    var = _group_sum(yc * yc, g64_ref[...])
    yn = yc * lax.rsqrt(var + GN_EPS) * gng_ref[...] + gnb_ref[...]
    bonus = _group_sum(r * k2 * rk_ref[...], ones64_ref[...]) * v
    y_ref[...] = ((yn + bonus) * g).astype(BF16).reshape(n_batch, tt, w_)


def _rwkv(p1, v_first, prm, use_vres):
    n_batch, seq, _ = p1.shape
    tt = TT_RWKV
    rows_all = n_batch * tt
    n_units = rows_all // CHUNK
    w_ = RWKV_WIDTH
    tok = lambda width: pl.BlockSpec((n_batch, tt, width), lambda i: (0, i, 0))
    vec = _const_spec((1, w_))
    in_specs = [tok(RWKV_EXT)]
    args = [p1]
    if use_vres:
        in_specs.append(tok(w_))
        args.append(v_first)
    in_specs += [_const_spec((1, RWKV_EXT)), vec, vec, _const_spec((LANES, 2 * w_)),
                 _const_spec((GATE_RANK, w_)), vec, vec, vec, vec, vec]
    args += [prm["mu"], prm["w0"], prm["a0"], prm["lora"], prm["gate_b"], prm["k_k"], prm["k_a"],
             prm["r_k"], prm["gn_g"], prm["gn_b"]]
    if use_vres:
        in_specs += [vec, _const_spec((LANES, w_))]
        args += [prm["v0"], prm["vres_b"]]
    in_specs += [_const_spec((MXU_DIM, MXU_DIM))] * 3
    args += [prm["g64"], prm["ones64"], prm["tri"]]
    out_specs = [tok(w_)]
    out_shape = [jax.ShapeDtypeStruct((n_batch, seq, w_), BF16)]
    if not use_vres:
        out_specs.append(tok(w_))
        out_shape.append(jax.ShapeDtypeStruct((n_batch, seq, w_), F32))
    act_bf16 = pltpu.VMEM((rows_all, w_), BF16)
    act_f32 = pltpu.VMEM((rows_all, w_), F32)
    scratch = [pltpu.VMEM((8, RWKV_EXT), F32),
               pltpu.VMEM((n_batch * HEAD_PAIRS, LANES, LANES), F32),
               act_bf16, act_bf16, act_bf16, act_bf16, act_bf16,
               pltpu.VMEM((n_units, w_, LANES), BF16),
               pltpu.VMEM((n_units, w_, LANES), F32),
               act_bf16, act_bf16, act_f32, act_f32, act_f32]
    out = pl.pallas_call(
        functools.partial(_rwkv_kernel, use_vres=use_vres, n_batch=n_batch),
        grid=(seq // tt,),
        in_specs=in_specs,
        out_specs=out_specs,
        out_shape=out_shape,
        scratch_shapes=scratch,
        compiler_params=_params(1),
        name="rwkv",
    )(*args)
    if use_vres:
        return out[0], v_first
    return out[0], out[1]


def _att_kernel(q_ref, k0_ref, k1_ref, k2_ref, v0_ref, v1_ref, v2_ref, bias_ref, o_ref):
    lane = lax.broadcasted_iota(jnp.int32, (1, LANES), 1)
    zero = jnp.zeros((), BF16)
    n_half = TQ_ATT // TH_ATT
    head_lanes = [(lane >= sub * ATT_HEAD_DIM) & (lane < (sub + 1) * ATT_HEAD_DIM) for sub in range(2)]
    den_lane = [(1 - sub) * ATT_HEAD_DIM for sub in range(2)]
    ones_col = [jnp.where(lane == den_lane[sub], 1.0, 0.0).astype(BF16) for sub in range(2)]
    for group in range(ATT_HEADS // 2 // PAIRS_PER_STAGE):
        units = [(hp, half, sub)
                 for hp in range(group * PAIRS_PER_STAGE, (group + 1) * PAIRS_PER_STAGE)
                 for half in range(n_half) for sub in range(2)]
        k_win, v_win = {}, {}
        for hp in range(group * PAIRS_PER_STAGE, (group + 1) * PAIRS_PER_STAGE):
            cols = slice(hp * LANES, (hp + 1) * LANES)
            k_win[hp] = jnp.concatenate([k0_ref[:, cols], k1_ref[:, cols], k2_ref[:, cols]], axis=0)
            v_win[hp] = jnp.concatenate([v0_ref[:, cols], v1_ref[:, cols], v2_ref[:, cols]], axis=0)
        scores = []
        for hp, half, sub in units:
            q2 = q_ref[half * TH_ATT:(half + 1) * TH_ATT, hp * LANES:(hp + 1) * LANES]
            k2 = k_win[hp][half * TH_ATT:half * TH_ATT + KH_ATT]
            scores.append(_dot_nt(jnp.where(head_lanes[sub], q2, zero), k2)
                          + bias_ref[0, half, 2 * hp + sub])
        probs = [jnp.exp2(s - jnp.max(s, axis=-1, keepdims=True)).astype(BF16) for s in scores]
        pvs = []
        for (hp, half, sub), e in zip(units, probs):
            v2 = v_win[hp][half * TH_ATT:half * TH_ATT + KH_ATT]
            pvs.append(_dot(e, jnp.where(head_lanes[sub], v2, zero) + ones_col[sub]))
        for j in range(0, len(units), 2):
            hp, half, _ = units[j]
            lo, hi = pvs[j], pvs[j + 1]
            out = jnp.where(head_lanes[0], lo / lo[:, den_lane[0]:den_lane[0] + 1],
                            hi / hi[:, den_lane[1]:den_lane[1] + 1])
            o_ref[half * TH_ATT:(half + 1) * TH_ATT, hp * LANES:(hp + 1) * LANES] = out.astype(BF16)


def _att(qkv, bias, seq):
    t = qkv.shape[0]
    tq = TQ_ATT
    nq = seq // tq
    blk = lambda col, back: pl.BlockSpec(
        (tq, ATT_WIDTH), lambda b, i: (b * nq + jnp.maximum(i - back, 0), col))
    bias_spec = pl.BlockSpec((1,) + bias.shape[1:], lambda b, i: (jnp.minimum(i, 2), 0, 0, 0, 0))
    return pl.pallas_call(
        _att_kernel,
        grid=(t // seq, nq),
        in_specs=[blk(0, 0), blk(1, 2), blk(1, 1), blk(1, 0), blk(2, 2), blk(2, 1), blk(2, 0), bias_spec],
        out_specs=pl.BlockSpec((tq, ATT_WIDTH), lambda b, i: (b * nq + i, 0)),
        out_shape=jax.ShapeDtypeStruct((t, ATT_WIDTH), BF16),
        compiler_params=_params(2),
        name="band_att",
    )(qkv, qkv, qkv, qkv, qkv, qkv, qkv, bias)


def _att_bias(rel_table):
    heads = rel_table.shape[0]
    far = rel_table[:, N_REL - 1:]
    near = rel_table[:, :1]
    n_far = KH_ATT - TH_ATT - REL_MAX
    period = KH_ATT + TH_ATT + 1
    base = jnp.concatenate([
        jnp.broadcast_to(far, (heads, n_far)),
        rel_table[:, ::-1],
        jnp.broadcast_to(near, (heads, KH_ATT - n_far - N_REL)),
        jnp.broadcast_to(far, (heads, period - KH_ATT)),
    ], axis=1)
    rows = jnp.tile(base, (1, TH_ATT))[:, :TH_ATT * (period - 1)].reshape(heads, TH_ATT, period - 1)
    bias = rows[:, :, :KH_ATT].astype(F32) * LOG2_E
    qc = jnp.arange(TH_ATT)[:, None] // CHUNK
    kcol = jnp.arange(KH_ATT)[None, :]
    in_band = (kcol // CHUNK >= qc) & (kcol // CHUNK <= qc + LEFT_CHUNKS)
    n_half = TQ_ATT // TH_ATT
    variants = []
    for tile in range(3):
        for half in range(n_half):
            first_valid = max((2 - tile) * TQ_ATT - half * TH_ATT, 0)
            variants.append(jnp.where((in_band & (kcol >= first_valid))[None], bias, NEG_INF))
    return jnp.stack(variants).reshape(3, n_half, heads, TH_ATT, KH_ATT)


def _memkv_kernel(mem_ref, g_ref, w_ref, kg_ref, g128_ref, mk_ref, mv_ref):
    h = _rms(mem_ref[...], g_ref[0]).astype(BF16)
    kv = _dot(h, w_ref[0])
    mk = kv[:, :MEM_WIDTH]
    ms = _group_sum(mk * mk, g128_ref[...])
    mk_ref[0] = (mk * lax.rsqrt(ms + RMS_EPS) * kg_ref[0]).astype(BF16)
    mv_ref[0] = kv[:, MEM_WIDTH:].astype(BF16)


def _memkv(mem, norm_mem, w_kv, k_gain, g128):
    m, d = mem.shape
    depth = w_kv.shape[0]
    return pl.pallas_call(
        _memkv_kernel,
        grid=(depth,),
        in_specs=[_const_spec((m, d)),
                  pl.BlockSpec((1, 1, d), lambda l: (l, 0, 0)),
                  pl.BlockSpec((1, d, 2 * MEM_WIDTH), lambda l: (l, 0, 0)),
                  pl.BlockSpec((1, 1, MEM_WIDTH), lambda l: (l, 0, 0)),
                  _const_spec(g128.shape)],
        out_specs=[pl.BlockSpec((1, m, MEM_WIDTH), lambda l: (l, 0, 0)),
                   pl.BlockSpec((1, m, MEM_WIDTH), lambda l: (l, 0, 0))],
        out_shape=[jax.ShapeDtypeStruct((depth, m, MEM_WIDTH), BF16),
                   jax.ShapeDtypeStruct((depth, m, MEM_WIDTH), BF16)],
        compiler_params=_params(1),
        name="mem_kv",
    )(mem, norm_mem.reshape(depth, 1, d), w_kv, k_gain.reshape(depth, 1, MEM_WIDTH), g128)


def _memattn_kernel(q_ref, k_ref, v_ref, o_ref):
    for h in range(MEM_HEADS):
        cols = slice(h * MEM_HEAD_DIM, (h + 1) * MEM_HEAD_DIM)
        s = _dot_nt(q_ref[:, cols], k_ref[:, cols])
        e = jnp.exp(s - jnp.max(s, axis=-1, keepdims=True))
        denom = jnp.sum(e, axis=-1, keepdims=True)
        o_ref[:, cols] = (_dot(e.astype(BF16), v_ref[:, cols]) / denom).astype(BF16)


def _memattn(mq, mk, mv, seq):
    t = mq.shape[0]
    tm = TM_MEM
    ns = seq // tm
    m = mk.shape[0] // (t // seq)
    kv_spec = pl.BlockSpec((m, MEM_WIDTH), lambda b, i: (b, 0))
    return pl.pallas_call(
        _memattn_kernel,
        grid=(t // seq, ns),
        in_specs=[pl.BlockSpec((tm, MEM_WIDTH), lambda b, i: (b * ns + i, 0)), kv_spec, kv_spec],
        out_specs=pl.BlockSpec((tm, MEM_WIDTH), lambda b, i: (b * ns + i, 0)),
        out_shape=jax.ShapeDtypeStruct((t, MEM_WIDTH), BF16),
        compiler_params=_params(2),
        name="mem_att",
    )(mq, mk, mv)


def _merge_kernel(x_ref, ng_ref, yr_ref, ya_ref, ym_ref, wg_ref, bg_ref, wr_ref, wa_ref, wm_ref,
                  wo_ref, o_ref):
    x = x_ref[...]
    d = x.shape[1]
    h = _rms(x, ng_ref[...]).astype(BF16)
    gates = jax.nn.sigmoid(_dot(h, wg_ref[...]) + bg_ref[...])
    merged = (gates[:, 0:d] * _dot(yr_ref[...], wr_ref[...])
              + gates[:, d:2 * d] * _dot(ya_ref[...], wa_ref[...])
              + gates[:, 2 * d:3 * d] * _dot(ym_ref[...], wm_ref[...]))
    o_ref[...] = x + _dot(merged.astype(BF16), wo_ref[...])


def _merge(x, gain, y_rwkv, y_att, y_mem, w_gate, b_gate, w_r, w_a, w_m, w_out, layer):
    t, d = x.shape
    tm = TM_MERGE
    tok = lambda width: pl.BlockSpec((tm, width), lambda i: (i, 0))
    return pl.pallas_call(
        _merge_kernel,
        grid=(t // tm,),
        in_specs=[tok(d), _const_spec((1, d)), tok(RWKV_WIDTH), tok(ATT_WIDTH), tok(MEM_WIDTH),
                  _layer_spec(w_gate.shape, layer), _const_spec((1, 3 * d)),
                  _layer_spec(w_r.shape, layer), _layer_spec(w_a.shape, layer),
                  _layer_spec(w_m.shape, layer), _layer_spec(w_out.shape, layer)],
        out_specs=tok(d),
        out_shape=jax.ShapeDtypeStruct((t, d), F32),
        compiler_params=_params(1),
        name="merge",
    )(x, gain.reshape(1, d), y_rwkv, y_att, y_mem, w_gate, b_gate.reshape(1, 3 * d), w_r, w_a, w_m,
      w_out)


def _block_diag_const(width, group, value):
    idx = jnp.arange(width) // group
    return jnp.where(idx[:, None] == idx[None, :], value, 0.0).astype(BF16)


def _chunk_tri(rows):
    i = jnp.arange(rows)
    same_chunk = (i[:, None] // CHUNK) == (i[None, :] // CHUNK)
    return (same_chunk & (i[None, :] <= i[:, None])).astype(BF16)


def kernel(x, mem, norm_ffn1, ffn1_w_in, ffn1_w_out, norm_mix, w_in, shift_mu, decay_w0, decay_lora_b, iclr_a0, iclr_lora_b, gate_lora_b, rwkv_k_k, rwkv_k_a, rwkv_r_k, rwkv_gn_g, rwkv_gn_b, vres_v0, vres_lora_a, vres_lora_b, att_q_norm, att_k_norm, att_rel_bias, norm_mem, mem_w_kv, mem_q_norm, mem_k_norm, w_branch_rwkv, w_branch_att, w_branch_mem, w_gate, b_gate, w_out, norm_ffn2, ffn2_w_in, ffn2_w_out):
    bsz, seq, d = x.shape
    t = bsz * seq
    w_ = RWKV_WIDTH
    depth = w_in.shape[0]
    xf = x.reshape(t, d)

    g64 = _block_diag_const(MXU_DIM, RWKV_HEAD_DIM, 1.0 / RWKV_HEAD_DIM)
    ones64 = _block_diag_const(MXU_DIM, RWKV_HEAD_DIM, 1.0)
    g128 = _block_diag_const(MXU_DIM, MEM_HEAD_DIM, 1.0 / MEM_HEAD_DIM)
    tri = _chunk_tri(MXU_DIM)

    mk_all, mv_all = _memkv(mem.reshape(bsz * mem.shape[1], d), norm_mem, mem_w_kv.astype(BF16),
                            jnp.tile(mem_k_norm, (1, MEM_HEADS)), g128)

    ffn1_in, ffn1_out = ffn1_w_in.astype(BF16), ffn1_w_out.astype(BF16)
    ffn2_in, ffn2_out = ffn2_w_in.astype(BF16), ffn2_w_out.astype(BF16)
    wg, wo = w_gate.astype(BF16), w_out.astype(BF16)
    wbr, wba, wbm = w_branch_rwkv.astype(BF16), w_branch_att.astype(BF16), w_branch_mem.astype(BF16)

    v_first = None
    for l in range(depth):
        xf = _ffn(xf, norm_ffn1[l], ffn1_in, ffn1_out, l)

        use_vres = l > 0
        w_l = w_in[l]
        vres_a = vres_lora_a[l - 1] if use_vres else jnp.zeros((d, VRES_RANK), F32)
        w1 = jnp.concatenate([w_l[:, :RWKV_IN], vres_a, jnp.zeros((d, LANES - VRES_RANK), F32)],
                             axis=1).astype(BF16)
        w2 = w_l[:, RWKV_IN:RWKV_IN + ATT_IN].astype(BF16)
        w3 = w_l[:, RWKV_IN + ATT_IN:].astype(BF16)
        qk_gain = jnp.concatenate([jnp.tile(att_q_norm[l], ATT_HEADS) * (ATT_HEAD_DIM ** -0.5 * LOG2_E),
                                   jnp.tile(att_k_norm[l], ATT_HEADS)]).reshape(1, 2 * ATT_WIDTH)
        mq_gain = (jnp.tile(mem_q_norm[l], MEM_HEADS) * (MEM_HEAD_DIM ** -0.5)).reshape(1, MEM_WIDTH)
        p1, qkv, mq = _proj(xf, norm_mix[l], w1, w2, w3, qk_gain, mq_gain, g64, g128)

        zeros_r = jnp.zeros((DECAY_RANK, w_), F32)
        prm = dict(
            mu=jnp.concatenate([shift_mu[l], jnp.zeros((LANES,), F32)]).reshape(1, RWKV_EXT),
            w0=decay_w0[l].reshape(1, w_), a0=iclr_a0[l].reshape(1, w_),
            lora=jnp.concatenate([jnp.concatenate([decay_lora_b[l], zeros_r], axis=1),
                                  jnp.concatenate([zeros_r, iclr_lora_b[l]], axis=1)], axis=0).astype(BF16),
            gate_b=gate_lora_b[l].astype(BF16),
            k_k=rwkv_k_k[l].reshape(1, w_), k_a=rwkv_k_a[l].reshape(1, w_),
            r_k=rwkv_r_k[l].reshape(1, w_), gn_g=rwkv_gn_g[l].reshape(1, w_),
            gn_b=rwkv_gn_b[l].reshape(1, w_), g64=g64, ones64=ones64, tri=tri)
        if use_vres:
            prm["v0"] = vres_v0[l - 1].reshape(1, w_)
            prm["vres_b"] = jnp.concatenate(
                [vres_lora_b[l - 1], jnp.zeros((LANES - VRES_RANK, w_), F32)], axis=0).astype(BF16)
        y_rwkv, v_first = _rwkv(p1.reshape(bsz, seq, RWKV_EXT), v_first, prm, use_vres)

        y_att = _att(qkv, _att_bias(att_rel_bias[l]), seq)
        y_mem = _memattn(mq, mk_all[l], mv_all[l], seq)

        xf = _merge(xf, norm_mix[l], y_rwkv.reshape(t, w_), y_att, y_mem, wg, b_gate[l], wbr, wba, wbm,
                    wo, l)
        xf = _ffn(xf, norm_ffn2[l], ffn2_in, ffn2_out, l)
    return xf.reshape(bsz, seq, d)
```

```python
import functools
import math

import jax
import jax.numpy as jnp
from jax import lax
from jax.experimental import pallas as pl
from jax.experimental.pallas import tpu as pltpu

F32 = jnp.float32
BF16 = jnp.bfloat16

V7X_VMEM_LIMIT_BYTES = 56 * 1024 * 1024

D_MODEL = 1024
DEPTH = 4
CHUNK = 64
RWKV_HEADS = 8
RWKV_HEAD_DIM = 64
RWKV_WIDTH = RWKV_HEADS * RWKV_HEAD_DIM
DECAY_RANK = 64
ICLR_RANK = 64
GATE_RANK = 128
VRES_RANK = 32
ATT_HEADS = 8
ATT_HEAD_DIM = 64
ATT_WIDTH = ATT_HEADS * ATT_HEAD_DIM
LEFT_CHUNKS = 8
REL_MIN = -(CHUNK - 1)
REL_MAX = 128
N_REL = REL_MAX - REL_MIN + 1
MEM_HEADS = 4
MEM_HEAD_DIM = 128
MEM_WIDTH = MEM_HEADS * MEM_HEAD_DIM
D_FF = 2816
RMS_EPS = 1e-6
GN_EPS = 64e-5
L2_EPS = 1e-12
NEG_INF = -1e30

RWKV_IN = 3 * RWKV_WIDTH + DECAY_RANK + ICLR_RANK + GATE_RANK
LANES = 128
MXU_DIM = 256
RWKV_EXT = RWKV_IN + LANES
ATT_IN = 3 * ATT_WIDTH
HEAD_PAIRS = RWKV_WIDTH // LANES

TM_FFN = 512
TM_PROJ = 512
TM_MERGE = 512
TT_RWKV = 256
TQ_ATT = 256
TH_ATT = 128
KH_ATT = TH_ATT + LEFT_CHUNKS * CHUNK
PAIRS_PER_STAGE = 4
LOG2_E = 1.4426950408889634
TM_MEM = 512


def _dot(a, b):
    return jnp.dot(a, b, preferred_element_type=F32)


def _dot_nt(a, b):
    return lax.dot_general(a, b, (((1,), (1,)), ((), ())), preferred_element_type=F32)


def _rms(x, gain):
    return x * lax.rsqrt(jnp.mean(x * x, axis=-1, keepdims=True) + RMS_EPS) * gain


def _group_sum(z, g):
    zb = z.astype(BF16)
    return jnp.concatenate([_dot(zb[:, i:i + MXU_DIM], g) for i in range(0, z.shape[1], MXU_DIM)],
                           axis=1)


def _const_spec(shape):
    nd = len(shape)
    return pl.BlockSpec(shape, lambda *_: (0,) * nd, pipeline_mode=pl.Buffered(1))


def _layer_spec(stacked_shape, layer):
    nd = len(stacked_shape) - 1
    return pl.BlockSpec((None,) + tuple(stacked_shape[1:]), lambda *_: (layer,) + (0,) * nd,
                        pipeline_mode=pl.Buffered(1))


def _params(n_axes):
    return pltpu.CompilerParams(dimension_semantics=("arbitrary",) * n_axes,
                                vmem_limit_bytes=V7X_VMEM_LIMIT_BYTES)


def _ffn_kernel(x_ref, g_ref, win_ref, wout_ref, o_ref):
    x = x_ref[...]
    h = _rms(x, g_ref[...]).astype(BF16)
    gu = _dot(h, win_ref[...])
    gate = gu[:, :D_FF]
    up = gu[:, D_FF:]
    act = (gate * jax.nn.sigmoid(gate) * up).astype(BF16)
    o_ref[...] = x + 0.5 * _dot(act, wout_ref[...])


def _ffn(x, gain, w_in, w_out, layer):
    t, d = x.shape
    return pl.pallas_call(
        _ffn_kernel,
        grid=(t // TM_FFN,),
        in_specs=[pl.BlockSpec((TM_FFN, d), lambda i: (i, 0)),
                  _const_spec((1, d)),
                  _layer_spec(w_in.shape, layer),
                  _layer_spec(w_out.shape, layer)],
        out_specs=pl.BlockSpec((TM_FFN, d), lambda i: (i, 0)),
        out_shape=jax.ShapeDtypeStruct((t, d), F32),
        compiler_params=_params(1),
        name="ffn",
    )(x, gain.reshape(1, d), w_in, w_out)


def _proj_kernel(x_ref, g_ref, w1_ref, w2_ref, w3_ref, qkg_ref, mqg_ref, g64_ref, g128_ref,
                 p1_ref, qkv_ref, mq_ref):
    h = _rms(x_ref[...], g_ref[...]).astype(BF16)
    p1_ref[...] = _dot(h, w1_ref[...])
    qkv = _dot(h, w2_ref[...])
    qk = qkv[:, :2 * ATT_WIDTH]
    ms = _group_sum(qk * qk, g64_ref[...])
    qkv_ref[:, :2 * ATT_WIDTH] = (qk * lax.rsqrt(ms + RMS_EPS) * qkg_ref[...]).astype(BF16)
    qkv_ref[:, 2 * ATT_WIDTH:] = qkv[:, 2 * ATT_WIDTH:].astype(BF16)
    mq = _dot(h, w3_ref[...])
    ms = _group_sum(mq * mq, g128_ref[...])
    mq_ref[...] = (mq * lax.rsqrt(ms + RMS_EPS) * mqg_ref[...]).astype(BF16)


def _proj(x, gain, w1, w2, w3, qk_gain, mq_gain, g64, g128):
    t, d = x.shape
    tm = TM_PROJ
    return pl.pallas_call(
        _proj_kernel,
        grid=(t // tm,),
        in_specs=[pl.BlockSpec((tm, d), lambda i: (i, 0)),
                  _const_spec((1, d)),
                  _const_spec(w1.shape), _const_spec(w2.shape), _const_spec(w3.shape),
                  _const_spec(qk_gain.shape), _const_spec(mq_gain.shape),
                  _const_spec(g64.shape), _const_spec(g128.shape)],
        out_specs=[pl.BlockSpec((tm, RWKV_EXT), lambda i: (i, 0)),
                   pl.BlockSpec((tm, ATT_IN), lambda i: (i, 0)),
                   pl.BlockSpec((tm, MEM_WIDTH), lambda i: (i, 0))],
        out_shape=[jax.ShapeDtypeStruct((t, RWKV_EXT), F32),
                   jax.ShapeDtypeStruct((t, ATT_IN), BF16),
                   jax.ShapeDtypeStruct((t, MEM_WIDTH), BF16)],
        compiler_params=_params(1),
        name="proj",
    )(x, gain.reshape(1, d), w1, w2, w3, qk_gain, mq_gain, g64, g128)


def _pair_blockdiag(x, lane_lo):
    zero = jnp.zeros((), x.dtype)
    return jnp.concatenate([jnp.where(lane_lo, x, zero), jnp.where(lane_lo, zero, x)], axis=0)


def _rwkv_kernel(*refs, use_vres, n_batch):
    if use_vres:
        (p_ref, vf_ref, mu_ref, w0_ref, a0_ref, lora_ref, gateb_ref, kk_ref, ka_ref, rk_ref,
         gng_ref, gnb_ref, v0_ref, vresb_ref, g64_ref, ones64_ref, tri_ref,
         y_ref,
         carry_ref, state_ref, rt_ref, at_ref, bt_ref, kt_ref, v_ref, bkt_ref, wcol_ref,
         tinv_ref, arb_ref, rhsl_ref, yl_ref, ys_ref) = refs
        vfo_ref = None
    else:
        (p_ref, mu_ref, w0_ref, a0_ref, lora_ref, gateb_ref, kk_ref, ka_ref, rk_ref,
         gng_ref, gnb_ref, g64_ref, ones64_ref, tri_ref,
         y_ref, vfo_ref,
         carry_ref, state_ref, rt_ref, at_ref, bt_ref, kt_ref, v_ref, bkt_ref, wcol_ref,
         tinv_ref, arb_ref, rhsl_ref, yl_ref, ys_ref) = refs
    tt = p_ref.shape[1]
    rows_all = n_batch * tt
    n_chunks = tt // CHUNK
    n_units = n_batch * n_chunks
    w_ = RWKV_WIDTH

    @pl.when(pl.program_id(0) == 0)
    def _():
        carry_ref[...] = jnp.zeros_like(carry_ref)
        state_ref[...] = jnp.zeros_like(state_ref)

    p = p_ref[...].reshape(rows_all, RWKV_EXT)
    row = lax.broadcasted_iota(jnp.int32, p.shape, 0)
    prev = pltpu.roll(p, 1, axis=0)
    for b in range(n_batch):
        prev = jnp.where(row == b * tt, carry_ref[b:b + 1, :], prev)
    for b in range(n_batch):
        carry_ref[b:b + 1, :] = p[(b + 1) * tt - 1:(b + 1) * tt, :]
    xs = p + (prev - p) * mu_ref[...]

    r = xs[:, 0:w_]
    k = xs[:, w_:2 * w_]
    v = xs[:, 2 * w_:3 * w_]
    lw = xs[:, 3 * w_:3 * w_ + LANES]
    xg = xs[:, 3 * w_ + LANES:3 * w_ + 2 * LANES]

    lane = lax.broadcasted_iota(jnp.int32, lw.shape, 1)
    z = jnp.where(lane < DECAY_RANK, jnp.tanh(lw), lw).astype(BF16)
    lo = _dot(z, lora_ref[...])
    logw = F32(-LOG2_E * math.exp(-0.5)) * jax.nn.sigmoid(w0_ref[...] + lo[:, :w_])
    a = jax.nn.sigmoid(a0_ref[...] + lo[:, w_:])
    g = _dot(jax.nn.sigmoid(xg).astype(BF16), gateb_ref[...])

    if use_vres:
        hv = xs[:, RWKV_IN:RWKV_EXT].astype(BF16)
        mix = jax.nn.sigmoid(v0_ref[...] + _dot(hv, vresb_ref[...]))
        v = v + (vf_ref[...].reshape(rows_all, w_) - v) * mix
    else:
        vfo_ref[...] = v.reshape(n_batch, tt, w_)

    kk = k * kk_ref[...]
    ss = _group_sum(kk * kk, ones64_ref[...])
    kk = kk * lax.rsqrt(jnp.maximum(ss, L2_EPS * L2_EPS))
    k2 = k * (1.0 + (a - 1.0) * ka_ref[...])
    bvec = kk * a

    tri = tri_ref[...]
    hi = logw.astype(BF16)
    mid = (logw - hi.astype(F32)).astype(BF16)
    cum = jnp.concatenate([_dot(tri, hi[i:i + MXU_DIM]) + _dot(tri, mid[i:i + MXU_DIM])
                           for i in range(0, rows_all, MXU_DIM)], axis=0)
    wcum = jnp.exp2(cum)
    inv_w = jnp.exp2(-cum)
    bt = bvec * inv_w
    kt = k2 * inv_w
    rt_ref[...] = (r * wcum).astype(BF16)
    at_ref[...] = (-kk * jnp.exp2(cum - logw)).astype(BF16)
    bt_ref[...] = bt.astype(BF16)
    kt_ref[...] = kt.astype(BF16)
    v_ref[...] = v.astype(BF16)
    for u in range(n_units):
        lo_row, hi_row = u * CHUNK, (u + 1) * CHUNK
        bkt_ref[u] = jnp.concatenate([bt[lo_row:hi_row], kt[lo_row:hi_row]], axis=0).T.astype(BF16)
        wcol_ref[u] = jnp.broadcast_to(wcum[hi_row - 1:hi_row, :], (LANES, w_)).T

    li = lax.broadcasted_iota(jnp.int32, (CHUNK, LANES), 0)
    mi = lax.broadcasted_iota(jnp.int32, (CHUNK, LANES), 1)
    lane_lo = mi < RWKV_HEAD_DIM
    mloc = jnp.where(lane_lo, mi, mi - RWKV_HEAD_DIM)
    strict = mloc < li
    incl = mloc <= li
    eye2 = (mloc == li).astype(F32)
    bi = lax.broadcasted_iota(jnp.int32, (LANES, LANES), 0)
    bj = lax.broadcasted_iota(jnp.int32, (LANES, LANES), 1)
    diag_blocks = (bi < RWKV_HEAD_DIM) == (bj < RWKV_HEAD_DIM)

    def tile_of(unit, pair):
        return (slice(unit * CHUNK, (unit + 1) * CHUNK), slice(pair * LANES, (pair + 1) * LANES))

    tiles = [tile_of(u, pr) for u in range(n_units) for pr in range(HEAD_PAIRS)]
    pw, tinv, grams = [], [], []
    for rows, cols in tiles:
        at2, rt2, bt2, kt2 = (ref[rows, cols] for ref in (at_ref, rt_ref, bt_ref, kt_ref))
        grams.append(_dot_nt(jnp.concatenate([at2, rt2], axis=0),
                             jnp.concatenate([_pair_blockdiag(bt2, lane_lo), _pair_blockdiag(kt2, lane_lo)],
                                             axis=0)))
    for (rows, cols), gram in zip(tiles, grams):
        v2 = v_ref[rows, cols]
        a_ab = jnp.where(strict, gram[:CHUNK, :LANES], 0.0)
        a_ak = jnp.where(strict, gram[:CHUNK, LANES:], 0.0)
        a_rb = jnp.where(incl, gram[CHUNK:, :LANES], 0.0)
        a_rk = jnp.where(incl, gram[CHUNK:, LANES:], 0.0)
        loc = _dot(jnp.concatenate([a_ak, a_rk], axis=0).astype(BF16), _pair_blockdiag(v2, lane_lo))
        rhsl_ref[rows, cols] = loc[:CHUNK]
        yl_ref[rows, cols] = loc[CHUNK:]
        arb_ref[rows, cols] = a_rb.astype(BF16)
        pw.append(a_ab)
        tinv.append(eye2 + a_ab)
    pwb = [x.astype(BF16) for x in pw]
    pw = [_dot(x, _pair_blockdiag(x, lane_lo)) for x in pwb]
    for level in range(5):
        pwb = [x.astype(BF16) for x in pw]
        if level < 4:
            both = [_dot(jnp.concatenate([x, tv.astype(BF16)], axis=0), _pair_blockdiag(x, lane_lo))
                    for x, tv in zip(pwb, tinv)]
            pw = [m[:CHUNK] for m in both]
            tinv = [tv + m[CHUNK:] for tv, m in zip(tinv, both)]
        else:
            tinv = [tv + _dot(tv.astype(BF16), _pair_blockdiag(x, lane_lo))
                    for x, tv in zip(pwb, tinv)]
    for (rows, cols), tv in zip(tiles, tinv):
        tinv_ref[rows, cols] = tv.astype(BF16)

    chains = [(b, pr) for b in range(n_batch) for pr in range(HEAD_PAIRS)]
    st = [state_ref[b * HEAD_PAIRS + pr] for b, pr in chains]
    for c in range(n_chunks):
        units = [b * n_chunks + c for b, _ in chains]
        tiles = [tile_of(u, pr) for u, (_, pr) in zip(units, chains)]
        m1 = [_dot(jnp.concatenate([at_ref[rows, cols], rt_ref[rows, cols]], axis=0), s.astype(BF16))
              for (rows, cols), s in zip(tiles, st)]
        rhs = [m[:CHUNK] + rhsl_ref[rows, cols] for m, (rows, cols) in zip(m1, tiles)]
        ub = [_dot(tinv_ref[rows, cols], _pair_blockdiag(x.astype(BF16), lane_lo)).astype(BF16)
              for x, (rows, cols) in zip(rhs, tiles)]
        new_st = []
        for u, (_, pr), (rows, cols), s, x in zip(units, chains, tiles, st, ub):
            chans = slice(pr * LANES, (pr + 1) * LANES)
            upd = _dot(bkt_ref[u, chans, :], jnp.concatenate([x, v_ref[rows, cols]], axis=0))
            new_st.append((s + jnp.where(diag_blocks, upd, 0.0)) * wcol_ref[u, chans, :])
        st = new_st
        for m, (rows, cols), x in zip(m1, tiles, ub):
            ys_ref[rows, cols] = (m[CHUNK:] + _dot(arb_ref[rows, cols], _pair_blockdiag(x, lane_lo))
                                  + yl_ref[rows, cols])
    for (b, pr), s in zip(chains, st):
        state_ref[b * HEAD_PAIRS + pr] = s

    y = ys_ref[...]
    mean = _group_sum(y, g64_ref[...])
    yc = y - mean
    var = _group_sum(yc * yc, g64_ref[...])
    yn = yc * lax.rsqrt(var + GN_EPS) * gng_ref[...] + gnb_ref[...]
    bonus = _group_sum(r * k2 * rk_ref[...], ones64_ref[...]) * v
    y_ref[...] = ((yn + bonus) * g).astype(BF16).reshape(n_batch, tt, w_)


def _rwkv(p1, v_first, prm, use_vres):
    n_batch, seq, _ = p1.shape
    tt = TT_RWKV
    rows_all = n_batch * tt
    n_units = rows_all // CHUNK
    w_ = RWKV_WIDTH
    tok = lambda width: pl.BlockSpec((n_batch, tt, width), lambda i: (0, i, 0))
    vec = _const_spec((1, w_))
    in_specs = [tok(RWKV_EXT)]
    args = [p1]
    if use_vres:
        in_specs.append(tok(w_))
        args.append(v_first)
    in_specs += [_const_spec((1, RWKV_EXT)), vec, vec, _const_spec((LANES, 2 * w_)),
                 _const_spec((GATE_RANK, w_)), vec, vec, vec, vec, vec]
    args += [prm["mu"], prm["w0"], prm["a0"], prm["lora"], prm["gate_b"], prm["k_k"], prm["k_a"],
             prm["r_k"], prm["gn_g"], prm["gn_b"]]
    if use_vres:
        in_specs += [vec, _const_spec((LANES, w_))]
        args += [prm["v0"], prm["vres_b"]]
    in_specs += [_const_spec((MXU_DIM, MXU_DIM))] * 3
    args += [prm["g64"], prm["ones64"], prm["tri"]]
    out_specs = [tok(w_)]
    out_shape = [jax.ShapeDtypeStruct((n_batch, seq, w_), BF16)]
    if not use_vres:
        out_specs.append(tok(w_))
        out_shape.append(jax.ShapeDtypeStruct((n_batch, seq, w_), F32))
    act_bf16 = pltpu.VMEM((rows_all, w_), BF16)
    act_f32 = pltpu.VMEM((rows_all, w_), F32)
    scratch = [pltpu.VMEM((8, RWKV_EXT), F32),
               pltpu.VMEM((n_batch * HEAD_PAIRS, LANES, LANES), F32),
               act_bf16, act_bf16, act_bf16, act_bf16, act_bf16,
               pltpu.VMEM((n_units, w_, LANES), BF16),
               pltpu.VMEM((n_units, w_, LANES), F32),
               act_bf16, act_bf16, act_f32, act_f32, act_f32]
    out = pl.pallas_call(
        functools.partial(_rwkv_kernel, use_vres=use_vres, n_batch=n_batch),
        grid=(seq // tt,),
        in_specs=in_specs,
        out_specs=out_specs,
        out_shape=out_shape,
        scratch_shapes=scratch,
        compiler_params=_params(1),
        name="rwkv",
    )(*args)
    if use_vres:
        return out[0], v_first
    return out[0], out[1]


def _att_kernel(q_ref, k0_ref, k1_ref, k2_ref, v0_ref, v1_ref, v2_ref, bias_ref, o_ref):
    lane = lax.broadcasted_iota(jnp.int32, (1, LANES), 1)
    zero = jnp.zeros((), BF16)
    n_half = TQ_ATT // TH_ATT
    head_lanes = [(lane >= sub * ATT_HEAD_DIM) & (lane < (sub + 1) * ATT_HEAD_DIM) for sub in range(2)]
    den_lane = [(1 - sub) * ATT_HEAD_DIM for sub in range(2)]
    ones_col = [jnp.where(lane == den_lane[sub], 1.0, 0.0).astype(BF16) for sub in range(2)]
    for group in range(ATT_HEADS // 2 // PAIRS_PER_STAGE):
        units = [(hp, half, sub)
                 for hp in range(group * PAIRS_PER_STAGE, (group + 1) * PAIRS_PER_STAGE)
                 for half in range(n_half) for sub in range(2)]
        k_win, v_win = {}, {}
        for hp in range(group * PAIRS_PER_STAGE, (group + 1) * PAIRS_PER_STAGE):
            cols = slice(hp * LANES, (hp + 1) * LANES)
            k_win[hp] = jnp.concatenate([k0_ref[:, cols], k1_ref[:, cols], k2_ref[:, cols]], axis=0)
            v_win[hp] = jnp.concatenate([v0_ref[:, cols], v1_ref[:, cols], v2_ref[:, cols]], axis=0)
        scores = []
        for hp, half, sub in units:
            q2 = q_ref[half * TH_ATT:(half + 1) * TH_ATT, hp * LANES:(hp + 1) * LANES]
            k2 = k_win[hp][half * TH_ATT:half * TH_ATT + KH_ATT]
            scores.append(_dot_nt(jnp.where(head_lanes[sub], q2, zero), k2)
                          + bias_ref[0, half, 2 * hp + sub])
        probs = [jnp.exp2(s - jnp.max(s, axis=-1, keepdims=True)).astype(BF16) for s in scores]
        pvs = []
        for (hp, half, sub), e in zip(units, probs):
            v2 = v_win[hp][half * TH_ATT:half * TH_ATT + KH_ATT]
            pvs.append(_dot(e, jnp.where(head_lanes[sub], v2, zero) + ones_col[sub]))
        for j in range(0, len(units), 2):
            hp, half, _ = units[j]
            lo, hi = pvs[j], pvs[j + 1]
            out = jnp.where(head_lanes[0], lo / lo[:, den_lane[0]:den_lane[0] + 1],
                            hi / hi[:, den_lane[1]:den_lane[1] + 1])
            o_ref[half * TH_ATT:(half + 1) * TH_ATT, hp * LANES:(hp + 1) * LANES] = out.astype(BF16)


def _att(qkv, bias, seq):
    t = qkv.shape[0]
    tq = TQ_ATT
    nq = seq // tq
    blk = lambda col, back: pl.BlockSpec(
        (tq, ATT_WIDTH), lambda b, i: (b * nq + jnp.maximum(i - back, 0), col))
    bias_spec = pl.BlockSpec((1,) + bias.shape[1:], lambda b, i: (jnp.minimum(i, 2), 0, 0, 0, 0))
    return pl.pallas_call(
        _att_kernel,
        grid=(t // seq, nq),
        in_specs=[blk(0, 0), blk(1, 2), blk(1, 1), blk(1, 0), blk(2, 2), blk(2, 1), blk(2, 0), bias_spec],
        out_specs=pl.BlockSpec((tq, ATT_WIDTH), lambda b, i: (b * nq + i, 0)),
        out_shape=jax.ShapeDtypeStruct((t, ATT_WIDTH), BF16),
        compiler_params=_params(2),
        name="band_att",
    )(qkv, qkv, qkv, qkv, qkv, qkv, qkv, bias)


def _att_bias(rel_table):
    heads = rel_table.shape[0]
    far = rel_table[:, N_REL - 1:]
    near = rel_table[:, :1]
    n_far = KH_ATT - TH_ATT - REL_MAX
    period = KH_ATT + TH_ATT + 1
    base = jnp.concatenate([
        jnp.broadcast_to(far, (heads, n_far)),
        rel_table[:, ::-1],
        jnp.broadcast_to(near, (heads, KH_ATT - n_far - N_REL)),
        jnp.broadcast_to(far, (heads, period - KH_ATT)),
    ], axis=1)
    rows = jnp.tile(base, (1, TH_ATT))[:, :TH_ATT * (period - 1)].reshape(heads, TH_ATT, period - 1)
    bias = rows[:, :, :KH_ATT].astype(F32) * LOG2_E
    qc = jnp.arange(TH_ATT)[:, None] // CHUNK
    kcol = jnp.arange(KH_ATT)[None, :]
    in_band = (kcol // CHUNK >= qc) & (kcol // CHUNK <= qc + LEFT_CHUNKS)
    n_half = TQ_ATT // TH_ATT
    variants = []
    for tile in range(3):
        for half in range(n_half):
            first_valid = max((2 - tile) * TQ_ATT - half * TH_ATT, 0)
            variants.append(jnp.where((in_band & (kcol >= first_valid))[None], bias, NEG_INF))
    return jnp.stack(variants).reshape(3, n_half, heads, TH_ATT, KH_ATT)


def _memkv_kernel(mem_ref, g_ref, w_ref, kg_ref, g128_ref, mk_ref, mv_ref):
    h = _rms(mem_ref[...], g_ref[0]).astype(BF16)
    kv = _dot(h, w_ref[0])
    mk = kv[:, :MEM_WIDTH]
    ms = _group_sum(mk * mk, g128_ref[...])
    mk_ref[0] = (mk * lax.rsqrt(ms + RMS_EPS) * kg_ref[0]).astype(BF16)
    mv_ref[0] = kv[:, MEM_WIDTH:].astype(BF16)


def _memkv(mem, norm_mem, w_kv, k_gain, g128):
    m, d = mem.shape
    depth = w_kv.shape[0]
    return pl.pallas_call(
        _memkv_kernel,
        grid=(depth,),
        in_specs=[_const_spec((m, d)),
                  pl.BlockSpec((1, 1, d), lambda l: (l, 0, 0)),
                  pl.BlockSpec((1, d, 2 * MEM_WIDTH), lambda l: (l, 0, 0)),
                  pl.BlockSpec((1, 1, MEM_WIDTH), lambda l: (l, 0, 0)),
                  _const_spec(g128.shape)],
        out_specs=[pl.BlockSpec((1, m, MEM_WIDTH), lambda l: (l, 0, 0)),
                   pl.BlockSpec((1, m, MEM_WIDTH), lambda l: (l, 0, 0))],
        out_shape=[jax.ShapeDtypeStruct((depth, m, MEM_WIDTH), BF16),
                   jax.ShapeDtypeStruct((depth, m, MEM_WIDTH), BF16)],
        compiler_params=_params(1),
        name="mem_kv",
    )(mem, norm_mem.reshape(depth, 1, d), w_kv, k_gain.reshape(depth, 1, MEM_WIDTH), g128)


def _memattn_kernel(q_ref, k_ref, v_ref, o_ref):
    heads = [slice(h * MEM_HEAD_DIM, (h + 1) * MEM_HEAD_DIM) for h in range(MEM_HEADS)]
    scores = [_dot_nt(q_ref[:, cols], k_ref[:, cols]) for cols in heads]
    probs = [jnp.exp(s - jnp.max(s, axis=-1, keepdims=True)) for s in scores]
    denoms = [jnp.sum(e, axis=-1, keepdims=True) for e in probs]
    pvs = [_dot(e.astype(BF16), v_ref[:, cols]) for e, cols in zip(probs, heads)]
    for cols, pv, denom in zip(heads, pvs, denoms):
        o_ref[:, cols] = (pv / denom).astype(BF16)


def _memattn(mq, mk, mv, seq):
    t = mq.shape[0]
    tm = TM_MEM
    ns = seq // tm
    m = mk.shape[0] // (t // seq)
    kv_spec = pl.BlockSpec((m, MEM_WIDTH), lambda b, i: (b, 0))
    return pl.pallas_call(
        _memattn_kernel,
        grid=(t // seq, ns),
        in_specs=[pl.BlockSpec((tm, MEM_WIDTH), lambda b, i: (b * ns + i, 0)), kv_spec, kv_spec],
        out_specs=pl.BlockSpec((tm, MEM_WIDTH), lambda b, i: (b * ns + i, 0)),
        out_shape=jax.ShapeDtypeStruct((t, MEM_WIDTH), BF16),
        compiler_params=_params(2),
        name="mem_att",
    )(mq, mk, mv)


def _merge_kernel(x_ref, ng_ref, yr_ref, ya_ref, ym_ref, wg_ref, bg_ref, wr_ref, wa_ref, wm_ref,
                  wo_ref, o_ref):
    x = x_ref[...]
    d = x.shape[1]
    h = _rms(x, ng_ref[...]).astype(BF16)
    gates = jax.nn.sigmoid(_dot(h, wg_ref[...]) + bg_ref[...])
    merged = (gates[:, 0:d] * _dot(yr_ref[...], wr_ref[...])
              + gates[:, d:2 * d] * _dot(ya_ref[...], wa_ref[...])
              + gates[:, 2 * d:3 * d] * _dot(ym_ref[...], wm_ref[...]))
    o_ref[...] = x + _dot(merged.astype(BF16), wo_ref[...])


def _merge(x, gain, y_rwkv, y_att, y_mem, w_gate, b_gate, w_r, w_a, w_m, w_out, layer):
    t, d = x.shape
    tm = TM_MERGE
    tok = lambda width: pl.BlockSpec((tm, width), lambda i: (i, 0))
    return pl.pallas_call(
        _merge_kernel,
        grid=(t // tm,),
        in_specs=[tok(d), _const_spec((1, d)), tok(RWKV_WIDTH), tok(ATT_WIDTH), tok(MEM_WIDTH),
                  _layer_spec(w_gate.shape, layer), _const_spec((1, 3 * d)),
                  _layer_spec(w_r.shape, layer), _layer_spec(w_a.shape, layer),
                  _layer_spec(w_m.shape, layer), _layer_spec(w_out.shape, layer)],
        out_specs=tok(d),
        out_shape=jax.ShapeDtypeStruct((t, d), F32),
        compiler_params=_params(1),
        name="merge",
    )(x, gain.reshape(1, d), y_rwkv, y_att, y_mem, w_gate, b_gate.reshape(1, 3 * d), w_r, w_a, w_m,
      w_out)


def _block_diag_const(width, group, value):
    idx = jnp.arange(width) // group
    return jnp.where(idx[:, None] == idx[None, :], value, 0.0).astype(BF16)


def _chunk_tri(rows):
    i = jnp.arange(rows)
    same_chunk = (i[:, None] // CHUNK) == (i[None, :] // CHUNK)
    return (same_chunk & (i[None, :] <= i[:, None])).astype(BF16)


def kernel(x, mem, norm_ffn1, ffn1_w_in, ffn1_w_out, norm_mix, w_in, shift_mu, decay_w0, decay_lora_b, iclr_a0, iclr_lora_b, gate_lora_b, rwkv_k_k, rwkv_k_a, rwkv_r_k, rwkv_gn_g, rwkv_gn_b, vres_v0, vres_lora_a, vres_lora_b, att_q_norm, att_k_norm, att_rel_bias, norm_mem, mem_w_kv, mem_q_norm, mem_k_norm, w_branch_rwkv, w_branch_att, w_branch_mem, w_gate, b_gate, w_out, norm_ffn2, ffn2_w_in, ffn2_w_out):
    bsz, seq, d = x.shape
    t = bsz * seq
    w_ = RWKV_WIDTH
    depth = w_in.shape[0]
    xf = x.reshape(t, d)

    g64 = _block_diag_const(MXU_DIM, RWKV_HEAD_DIM, 1.0 / RWKV_HEAD_DIM)
    ones64 = _block_diag_const(MXU_DIM, RWKV_HEAD_DIM, 1.0)
    g128 = _block_diag_const(MXU_DIM, MEM_HEAD_DIM, 1.0 / MEM_HEAD_DIM)
    tri = _chunk_tri(MXU_DIM)

    mk_all, mv_all = _memkv(mem.reshape(bsz * mem.shape[1], d), norm_mem, mem_w_kv.astype(BF16),
                            jnp.tile(mem_k_norm, (1, MEM_HEADS)), g128)

    ffn1_in, ffn1_out = ffn1_w_in.astype(BF16), ffn1_w_out.astype(BF16)
    ffn2_in, ffn2_out = ffn2_w_in.astype(BF16), ffn2_w_out.astype(BF16)
    wg, wo = w_gate.astype(BF16), w_out.astype(BF16)
    wbr, wba, wbm = w_branch_rwkv.astype(BF16), w_branch_att.astype(BF16), w_branch_mem.astype(BF16)

    v_first = None
    for l in range(depth):
        xf = _ffn(xf, norm_ffn1[l], ffn1_in, ffn1_out, l)

        use_vres = l > 0
        w_l = w_in[l]
        vres_a = vres_lora_a[l - 1] if use_vres else jnp.zeros((d, VRES_RANK), F32)
        w1 = jnp.concatenate([w_l[:, :RWKV_IN], vres_a, jnp.zeros((d, LANES - VRES_RANK), F32)],
                             axis=1).astype(BF16)
        w2 = w_l[:, RWKV_IN:RWKV_IN + ATT_IN].astype(BF16)
        w3 = w_l[:, RWKV_IN + ATT_IN:].astype(BF16)
        qk_gain = jnp.concatenate([jnp.tile(att_q_norm[l], ATT_HEADS) * (ATT_HEAD_DIM ** -0.5 * LOG2_E),
                                   jnp.tile(att_k_norm[l], ATT_HEADS)]).reshape(1, 2 * ATT_WIDTH)
        mq_gain = (jnp.tile(mem_q_norm[l], MEM_HEADS) * (MEM_HEAD_DIM ** -0.5)).reshape(1, MEM_WIDTH)
        p1, qkv, mq = _proj(xf, norm_mix[l], w1, w2, w3, qk_gain, mq_gain, g64, g128)

        zeros_r = jnp.zeros((DECAY_RANK, w_), F32)
        prm = dict(
            mu=jnp.concatenate([shift_mu[l], jnp.zeros((LANES,), F32)]).reshape(1, RWKV_EXT),
            w0=decay_w0[l].reshape(1, w_), a0=iclr_a0[l].reshape(1, w_),
            lora=jnp.concatenate([jnp.concatenate([decay_lora_b[l], zeros_r], axis=1),
                                  jnp.concatenate([zeros_r, iclr_lora_b[l]], axis=1)], axis=0).astype(BF16),
            gate_b=gate_lora_b[l].astype(BF16),
            k_k=rwkv_k_k[l].reshape(1, w_), k_a=rwkv_k_a[l].reshape(1, w_),
            r_k=rwkv_r_k[l].reshape(1, w_), gn_g=rwkv_gn_g[l].reshape(1, w_),
            gn_b=rwkv_gn_b[l].reshape(1, w_), g64=g64, ones64=ones64, tri=tri)
        if use_vres:
            prm["v0"] = vres_v0[l - 1].reshape(1, w_)
            prm["vres_b"] = jnp.concatenate(
                [vres_lora_b[l - 1], jnp.zeros((LANES - VRES_RANK, w_), F32)], axis=0).astype(BF16)
        y_rwkv, v_first = _rwkv(p1.reshape(bsz, seq, RWKV_EXT), v_first, prm, use_vres)

        y_att = _att(qkv, _att_bias(att_rel_bias[l]), seq)
        y_mem = _memattn(mq, mk_all[l], mv_all[l], seq)

        xf = _merge(xf, norm_mix[l], y_rwkv.reshape(t, w_), y_att, y_mem, wg, b_gate[l], wbr, wba, wbm,
                    wo, l)
        xf = _ffn(xf, norm_ffn2[l], ffn2_in, ffn2_out, l)
    return xf.reshape(bsz, seq, d)
```

```python
import functools
import math

import jax
import jax.numpy as jnp
from jax import lax
from jax.experimental import pallas as pl
from jax.experimental.pallas import tpu as pltpu

F32 = jnp.float32
BF16 = jnp.bfloat16

V7X_VMEM_LIMIT_BYTES = 56 * 1024 * 1024

D_MODEL = 1024
DEPTH = 4
CHUNK = 64
RWKV_HEADS = 8
RWKV_HEAD_DIM = 64
RWKV_WIDTH = RWKV_HEADS * RWKV_HEAD_DIM
DECAY_RANK = 64
ICLR_RANK = 64
GATE_RANK = 128
VRES_RANK = 32
ATT_HEADS = 8
ATT_HEAD_DIM = 64
ATT_WIDTH = ATT_HEADS * ATT_HEAD_DIM
LEFT_CHUNKS = 8
REL_MIN = -(CHUNK - 1)
REL_MAX = 128
N_REL = REL_MAX - REL_MIN + 1
MEM_HEADS = 4
MEM_HEAD_DIM = 128
MEM_WIDTH = MEM_HEADS * MEM_HEAD_DIM
D_FF = 2816
RMS_EPS = 1e-6
GN_EPS = 64e-5
L2_EPS = 1e-12
NEG_INF = -1e30

RWKV_IN = 3 * RWKV_WIDTH + DECAY_RANK + ICLR_RANK + GATE_RANK
LANES = 128
MXU_DIM = 256
RWKV_EXT = RWKV_IN + LANES
ATT_IN = 3 * ATT_WIDTH
HEAD_PAIRS = RWKV_WIDTH // LANES

TM_FFN = 512
SUB_TILES = 2
TM_PROJ = 512
TM_MERGE = 512
TT_RWKV = 256
TQ_ATT = 256
TH_ATT = 128
KH_ATT = TH_ATT + LEFT_CHUNKS * CHUNK
PAIRS_PER_STAGE = 4
LOG2_E = 1.4426950408889634
TM_MEM = 2048


def _dot(a, b):
    return jnp.dot(a, b, preferred_element_type=F32)


def _dot_nt(a, b):
    return lax.dot_general(a, b, (((1,), (1,)), ((), ())), preferred_element_type=F32)


def _rms(x, gain):
    return x * lax.rsqrt(jnp.mean(x * x, axis=-1, keepdims=True) + RMS_EPS) * gain


def _group_sum(z, g):
    zb = z.astype(BF16)
    return jnp.concatenate([_dot(zb[:, i:i + MXU_DIM], g) for i in range(0, z.shape[1], MXU_DIM)],
                           axis=1)


def _const_spec(shape):
    nd = len(shape)
    return pl.BlockSpec(shape, lambda *_: (0,) * nd, pipeline_mode=pl.Buffered(1))


def _layer_spec(stacked_shape, layer):
    nd = len(stacked_shape) - 1
    return pl.BlockSpec((None,) + tuple(stacked_shape[1:]), lambda *_: (layer,) + (0,) * nd,
                        pipeline_mode=pl.Buffered(1))


def _params(n_axes):
    return pltpu.CompilerParams(dimension_semantics=("arbitrary",) * n_axes,
                                vmem_limit_bytes=V7X_VMEM_LIMIT_BYTES)


def _for_row_tiles(n_rows, tile_rows, body):
    def step(j, carry):
        body(pl.ds(pl.multiple_of(j * tile_rows, tile_rows), tile_rows))
        return carry

    lax.fori_loop(0, n_rows // tile_rows, step, 0)


def _ffn_kernel(x_ref, g_ref, win_ref, wout_ref, o_ref):
    def tile(rows):
        x = x_ref[rows, :]
        h = _rms(x, g_ref[...]).astype(BF16)
        gu = _dot(h, win_ref[...])
        gate = gu[:, :D_FF]
        up = gu[:, D_FF:]
        act = (gate * jax.nn.sigmoid(gate) * up).astype(BF16)
        o_ref[rows, :] = x + 0.5 * _dot(act, wout_ref[...])

    _for_row_tiles(x_ref.shape[0], TM_FFN, tile)


def _ffn(x, gain, w_in, w_out, layer):
    t, d = x.shape
    return pl.pallas_call(
        _ffn_kernel,
        grid=(t // (TM_FFN * SUB_TILES),),
        in_specs=[pl.BlockSpec((TM_FFN * SUB_TILES, d), lambda i: (i, 0)),
                  _const_spec((1, d)),
                  _layer_spec(w_in.shape, layer),
                  _layer_spec(w_out.shape, layer)],
        out_specs=pl.BlockSpec((TM_FFN * SUB_TILES, d), lambda i: (i, 0)),
        out_shape=jax.ShapeDtypeStruct((t, d), F32),
        compiler_params=_params(1),
        name="ffn",
    )(x, gain.reshape(1, d), w_in, w_out)


def _proj_kernel(x_ref, g_ref, w1_ref, w2_ref, w3_ref, qkg_ref, mqg_ref, g64_ref, g128_ref,
                 p1_ref, qkv_ref, mq_ref):
    h = _rms(x_ref[...], g_ref[...]).astype(BF16)
    p1_ref[...] = _dot(h, w1_ref[...])
    qkv = _dot(h, w2_ref[...])
    qk = qkv[:, :2 * ATT_WIDTH]
    ms = _group_sum(qk * qk, g64_ref[...])
    qkv_ref[:, :2 * ATT_WIDTH] = (qk * lax.rsqrt(ms + RMS_EPS) * qkg_ref[...]).astype(BF16)
    qkv_ref[:, 2 * ATT_WIDTH:] = qkv[:, 2 * ATT_WIDTH:].astype(BF16)
    mq = _dot(h, w3_ref[...])
    ms = _group_sum(mq * mq, g128_ref[...])
    mq_ref[...] = (mq * lax.rsqrt(ms + RMS_EPS) * mqg_ref[...]).astype(BF16)


def _proj(x, gain, w1, w2, w3, qk_gain, mq_gain, g64, g128):
    t, d = x.shape
    tm = TM_PROJ
    return pl.pallas_call(
        _proj_kernel,
        grid=(t // tm,),
        in_specs=[pl.BlockSpec((tm, d), lambda i: (i, 0)),
                  _const_spec((1, d)),
                  _const_spec(w1.shape), _const_spec(w2.shape), _const_spec(w3.shape),
                  _const_spec(qk_gain.shape), _const_spec(mq_gain.shape),
                  _const_spec(g64.shape), _const_spec(g128.shape)],
        out_specs=[pl.BlockSpec((tm, RWKV_EXT), lambda i: (i, 0)),
                   pl.BlockSpec((tm, ATT_IN), lambda i: (i, 0)),
                   pl.BlockSpec((tm, MEM_WIDTH), lambda i: (i, 0))],
        out_shape=[jax.ShapeDtypeStruct((t, RWKV_EXT), F32),
                   jax.ShapeDtypeStruct((t, ATT_IN), BF16),
                   jax.ShapeDtypeStruct((t, MEM_WIDTH), BF16)],
        compiler_params=_params(1),
        name="proj",
    )(x, gain.reshape(1, d), w1, w2, w3, qk_gain, mq_gain, g64, g128)


def _pair_blockdiag(x, lane_lo):
    zero = jnp.zeros((), x.dtype)
    return jnp.concatenate([jnp.where(lane_lo, x, zero), jnp.where(lane_lo, zero, x)], axis=0)


def _rwkv_kernel(*refs, use_vres, n_batch):
    if use_vres:
        (p_ref, vf_ref, mu_ref, w0_ref, a0_ref, lora_ref, gateb_ref, kk_ref, ka_ref, rk_ref,
         gng_ref, gnb_ref, v0_ref, vresb_ref, g64_ref, ones64_ref, tri_ref,
         y_ref,
         carry_ref, state_ref, rt_ref, at_ref, bt_ref, kt_ref, v_ref, bkt_ref, wcol_ref,
         tat_ref, arb_ref, ul_ref, yl_ref, ys_ref) = refs
        vfo_ref = None
    else:
        (p_ref, mu_ref, w0_ref, a0_ref, lora_ref, gateb_ref, kk_ref, ka_ref, rk_ref,
         gng_ref, gnb_ref, g64_ref, ones64_ref, tri_ref,
         y_ref, vfo_ref,
         carry_ref, state_ref, rt_ref, at_ref, bt_ref, kt_ref, v_ref, bkt_ref, wcol_ref,
         tat_ref, arb_ref, ul_ref, yl_ref, ys_ref) = refs
    tt = p_ref.shape[1]
    rows_all = n_batch * tt
    n_chunks = tt // CHUNK
    n_units = n_batch * n_chunks
    w_ = RWKV_WIDTH

    @pl.when(pl.program_id(0) == 0)
    def _():
        carry_ref[...] = jnp.zeros_like(carry_ref)
        state_ref[...] = jnp.zeros_like(state_ref)

    p = p_ref[...].reshape(rows_all, RWKV_EXT)
    row = lax.broadcasted_iota(jnp.int32, p.shape, 0)
    prev = pltpu.roll(p, 1, axis=0)
    for b in range(n_batch):
        prev = jnp.where(row == b * tt, carry_ref[b:b + 1, :], prev)
    for b in range(n_batch):
        carry_ref[b:b + 1, :] = p[(b + 1) * tt - 1:(b + 1) * tt, :]
    xs = p + (prev - p) * mu_ref[...]

    r = xs[:, 0:w_]
    k = xs[:, w_:2 * w_]
    v = xs[:, 2 * w_:3 * w_]
    lw = xs[:, 3 * w_:3 * w_ + LANES]
    xg = xs[:, 3 * w_ + LANES:3 * w_ + 2 * LANES]

    lane = lax.broadcasted_iota(jnp.int32, lw.shape, 1)
    z = jnp.where(lane < DECAY_RANK, jnp.tanh(lw), lw).astype(BF16)
    lo = _dot(z, lora_ref[...])
    logw = F32(-LOG2_E * math.exp(-0.5)) * jax.nn.sigmoid(w0_ref[...] + lo[:, :w_])
    a = jax.nn.sigmoid(a0_ref[...] + lo[:, w_:])
    g = _dot(jax.nn.sigmoid(xg).astype(BF16), gateb_ref[...])

    if use_vres:
        hv = xs[:, RWKV_IN:RWKV_EXT].astype(BF16)
        mix = jax.nn.sigmoid(v0_ref[...] + _dot(hv, vresb_ref[...]))
        v = v + (vf_ref[...].reshape(rows_all, w_) - v) * mix
    else:
        vfo_ref[...] = v.reshape(n_batch, tt, w_)

    kk = k * kk_ref[...]
    ss = _group_sum(kk * kk, ones64_ref[...])
    kk = kk * lax.rsqrt(jnp.maximum(ss, L2_EPS * L2_EPS))
    k2 = k * (1.0 + (a - 1.0) * ka_ref[...])
    bvec = kk * a

    tri = tri_ref[...]
    hi = logw.astype(BF16)
    mid = (logw - hi.astype(F32)).astype(BF16)
    cum = jnp.concatenate([_dot(tri, hi[i:i + MXU_DIM]) + _dot(tri, mid[i:i + MXU_DIM])
                           for i in range(0, rows_all, MXU_DIM)], axis=0)
    wcum = jnp.exp2(cum)
    inv_w = jnp.exp2(-cum)
    bt = bvec * inv_w
    kt = k2 * inv_w
    rt_ref[...] = (r * wcum).astype(BF16)
    at_ref[...] = (-kk * jnp.exp2(cum - logw)).astype(BF16)
    bt_ref[...] = bt.astype(BF16)
    kt_ref[...] = kt.astype(BF16)
    v_ref[...] = v.astype(BF16)
    for u in range(n_units):
        lo_row, hi_row = u * CHUNK, (u + 1) * CHUNK
        bkt_ref[u] = jnp.concatenate([bt[lo_row:hi_row], kt[lo_row:hi_row]], axis=0).T.astype(BF16)
        wcol_ref[u] = jnp.broadcast_to(wcum[hi_row - 1:hi_row, :], (LANES, w_)).T

    li = lax.broadcasted_iota(jnp.int32, (CHUNK, LANES), 0)
    mi = lax.broadcasted_iota(jnp.int32, (CHUNK, LANES), 1)
    lane_lo = mi < RWKV_HEAD_DIM
    mloc = jnp.where(lane_lo, mi, mi - RWKV_HEAD_DIM)
    strict = mloc < li
    incl = mloc <= li
    eye2 = (mloc == li).astype(F32)
    bi = lax.broadcasted_iota(jnp.int32, (LANES, LANES), 0)
    bj = lax.broadcasted_iota(jnp.int32, (LANES, LANES), 1)
    diag_blocks = (bi < RWKV_HEAD_DIM) == (bj < RWKV_HEAD_DIM)

    def tile_of(unit, pair):
        return (slice(unit * CHUNK, (unit + 1) * CHUNK), slice(pair * LANES, (pair + 1) * LANES))

    tiles = [tile_of(u, pr) for u in range(n_units) for pr in range(HEAD_PAIRS)]
    pw, tinv, grams, rhsl = [], [], [], []
    for rows, cols in tiles:
        at2, rt2, bt2, kt2 = (ref[rows, cols] for ref in (at_ref, rt_ref, bt_ref, kt_ref))
        grams.append(_dot_nt(jnp.concatenate([at2, rt2], axis=0),
                             jnp.concatenate([_pair_blockdiag(bt2, lane_lo), _pair_blockdiag(kt2, lane_lo)],
                                             axis=0)))
    for (rows, cols), gram in zip(tiles, grams):
        v2 = v_ref[rows, cols]
        a_ab = jnp.where(strict, gram[:CHUNK, :LANES], 0.0)
        a_ak = jnp.where(strict, gram[:CHUNK, LANES:], 0.0)
        a_rb = jnp.where(incl, gram[CHUNK:, :LANES], 0.0)
        a_rk = jnp.where(incl, gram[CHUNK:, LANES:], 0.0)
        loc = _dot(jnp.concatenate([a_ak, a_rk], axis=0).astype(BF16), _pair_blockdiag(v2, lane_lo))
        rhsl.append(loc[:CHUNK].astype(BF16))
        yl_ref[rows, cols] = loc[CHUNK:]
        arb_ref[rows, cols] = a_rb.astype(BF16)
        pw.append(a_ab)
        tinv.append(eye2 + a_ab)
    pwb = [x.astype(BF16) for x in pw]
    pw = [_dot(x, _pair_blockdiag(x, lane_lo)) for x in pwb]
    for level in range(5):
        pwb = [x.astype(BF16) for x in pw]
        if level < 4:
            both = [_dot(jnp.concatenate([x, tv.astype(BF16)], axis=0), _pair_blockdiag(x, lane_lo))
                    for x, tv in zip(pwb, tinv)]
            pw = [m[:CHUNK] for m in both]
            tinv = [tv + m[CHUNK:] for tv, m in zip(tinv, both)]
        else:
            tinv = [tv + _dot(tv.astype(BF16), _pair_blockdiag(x, lane_lo))
                    for x, tv in zip(pwb, tinv)]
    for (rows, cols), tv, x in zip(tiles, tinv, rhsl):
        both = _dot(tv.astype(BF16), jnp.concatenate([_pair_blockdiag(at_ref[rows, cols], lane_lo),
                                                      _pair_blockdiag(x, lane_lo)], axis=1))
        tat_ref[rows, cols] = both[:, :LANES].astype(BF16)
        ul_ref[rows, cols] = both[:, LANES:]

    chains = [(b, pr) for b in range(n_batch) for pr in range(HEAD_PAIRS)]
    st = [state_ref[b * HEAD_PAIRS + pr] for b, pr in chains]
    for c in range(n_chunks):
        units = [b * n_chunks + c for b, _ in chains]
        tiles = [tile_of(u, pr) for u, (_, pr) in zip(units, chains)]
        m1 = [_dot(jnp.concatenate([tat_ref[rows, cols], rt_ref[rows, cols]], axis=0), s.astype(BF16))
              for (rows, cols), s in zip(tiles, st)]
        ub = [(m[:CHUNK] + ul_ref[rows, cols]).astype(BF16) for m, (rows, cols) in zip(m1, tiles)]
        new_st = []
        for u, (_, pr), (rows, cols), s, x in zip(units, chains, tiles, st, ub):
            chans = slice(pr * LANES, (pr + 1) * LANES)
            upd = _dot(bkt_ref[u, chans, :], jnp.concatenate([x, v_ref[rows, cols]], axis=0))
            new_st.append((s + jnp.where(diag_blocks, upd, 0.0)) * wcol_ref[u, chans, :])
        st = new_st
        for m, (rows, cols), x in zip(m1, tiles, ub):
            ys_ref[rows, cols] = (m[CHUNK:] + _dot(arb_ref[rows, cols], _pair_blockdiag(x, lane_lo))
                                  + yl_ref[rows, cols])
    for (b, pr), s in zip(chains, st):
        state_ref[b * HEAD_PAIRS + pr] = s

    y = ys_ref[...]
    mean = _group_sum(y, g64_ref[...])
    yc = y - mean
    var = _group_sum(yc * yc, g64_ref[...])
    yn = yc * lax.rsqrt(var + GN_EPS) * gng_ref[...] + gnb_ref[...]
    bonus = _group_sum(r * k2 * rk_ref[...], ones64_ref[...]) * v
    y_ref[...] = ((yn + bonus) * g).astype(BF16).reshape(n_batch, tt, w_)


def _rwkv(p1, v_first, prm, use_vres):
    n_batch, seq, _ = p1.shape
    tt = TT_RWKV
    rows_all = n_batch * tt
    n_units = rows_all // CHUNK
    w_ = RWKV_WIDTH
    tok = lambda width: pl.BlockSpec((n_batch, tt, width), lambda i: (0, i, 0))
    vec = _const_spec((1, w_))
    in_specs = [tok(RWKV_EXT)]
    args = [p1]
    if use_vres:
        in_specs.append(tok(w_))
        args.append(v_first)
    in_specs += [_const_spec((1, RWKV_EXT)), vec, vec, _const_spec((LANES, 2 * w_)),
                 _const_spec((GATE_RANK, w_)), vec, vec, vec, vec, vec]
    args += [prm["mu"], prm["w0"], prm["a0"], prm["lora"], prm["gate_b"], prm["k_k"], prm["k_a"],
             prm["r_k"], prm["gn_g"], prm["gn_b"]]
    if use_vres:
        in_specs += [vec, _const_spec((LANES, w_))]
        args += [prm["v0"], prm["vres_b"]]
    in_specs += [_const_spec((MXU_DIM, MXU_DIM))] * 3
    args += [prm["g64"], prm["ones64"], prm["tri"]]
    out_specs = [tok(w_)]
    out_shape = [jax.ShapeDtypeStruct((n_batch, seq, w_), BF16)]
    if not use_vres:
        out_specs.append(tok(w_))
        out_shape.append(jax.ShapeDtypeStruct((n_batch, seq, w_), F32))
    act_bf16 = pltpu.VMEM((rows_all, w_), BF16)
    act_f32 = pltpu.VMEM((rows_all, w_), F32)
    scratch = [pltpu.VMEM((8, RWKV_EXT), F32),
               pltpu.VMEM((n_batch * HEAD_PAIRS, LANES, LANES), F32),
               act_bf16, act_bf16, act_bf16, act_bf16, act_bf16,
               pltpu.VMEM((n_units, w_, LANES), BF16),
               pltpu.VMEM((n_units, w_, LANES), F32),
               act_bf16, act_bf16, act_f32, act_f32, act_f32]
    out = pl.pallas_call(
        functools.partial(_rwkv_kernel, use_vres=use_vres, n_batch=n_batch),
        grid=(seq // tt,),
        in_specs=in_specs,
        out_specs=out_specs,
        out_shape=out_shape,
        scratch_shapes=scratch,
        compiler_params=_params(1),
        name="rwkv",
    )(*args)
    if use_vres:
        return out[0], v_first
    return out[0], out[1]


def _att_kernel(q_ref, k0_ref, k1_ref, k2_ref, v0_ref, v1_ref, v2_ref, bias_ref, o_ref):
    lane = lax.broadcasted_iota(jnp.int32, (1, LANES), 1)
    zero = jnp.zeros((), BF16)
    n_half = TQ_ATT // TH_ATT
    head_lanes = [(lane >= sub * ATT_HEAD_DIM) & (lane < (sub + 1) * ATT_HEAD_DIM) for sub in range(2)]
    den_lane = [(1 - sub) * ATT_HEAD_DIM for sub in range(2)]
    ones_col = [jnp.where(lane == den_lane[sub], 1.0, 0.0).astype(BF16) for sub in range(2)]
    for group in range(ATT_HEADS // 2 // PAIRS_PER_STAGE):
        units = [(hp, half, sub)
                 for hp in range(group * PAIRS_PER_STAGE, (group + 1) * PAIRS_PER_STAGE)
                 for half in range(n_half) for sub in range(2)]
        k_win, v_win = {}, {}
        for hp in range(group * PAIRS_PER_STAGE, (group + 1) * PAIRS_PER_STAGE):
            cols = slice(hp * LANES, (hp + 1) * LANES)
            k_win[hp] = jnp.concatenate([k0_ref[:, cols], k1_ref[:, cols], k2_ref[:, cols]], axis=0)
            v_win[hp] = jnp.concatenate([v0_ref[:, cols], v1_ref[:, cols], v2_ref[:, cols]], axis=0)
        scores = []
        for hp, half, sub in units:
            q2 = q_ref[half * TH_ATT:(half + 1) * TH_ATT, hp * LANES:(hp + 1) * LANES]
            k2 = k_win[hp][half * TH_ATT:half * TH_ATT + KH_ATT]
            scores.append(_dot_nt(jnp.where(head_lanes[sub], q2, zero), k2)
                          + bias_ref[0, half, 2 * hp + sub])
        probs = [jnp.exp2(s - jnp.max(s, axis=-1, keepdims=True)).astype(BF16) for s in scores]
        pvs = []
        for (hp, half, sub), e in zip(units, probs):
            v2 = v_win[hp][half * TH_ATT:half * TH_ATT + KH_ATT]
            pvs.append(_dot(e, jnp.where(head_lanes[sub], v2, zero) + ones_col[sub]))
        for j in range(0, len(units), 2):
            hp, half, _ = units[j]
            lo, hi = pvs[j], pvs[j + 1]
            out = jnp.where(head_lanes[0], lo / lo[:, den_lane[0]:den_lane[0] + 1],
                            hi / hi[:, den_lane[1]:den_lane[1] + 1])
            o_ref[half * TH_ATT:(half + 1) * TH_ATT, hp * LANES:(hp + 1) * LANES] = out.astype(BF16)


def _att(qkv, bias, seq):
    t = qkv.shape[0]
    tq = TQ_ATT
    nq = seq // tq
    blk = lambda col, back: pl.BlockSpec(
        (tq, ATT_WIDTH), lambda b, i: (b * nq + jnp.maximum(i - back, 0), col))
    bias_spec = pl.BlockSpec((1,) + bias.shape[1:], lambda b, i: (jnp.minimum(i, 2), 0, 0, 0, 0))
    return pl.pallas_call(
        _att_kernel,
        grid=(t // seq, nq),
        in_specs=[blk(0, 0), blk(1, 2), blk(1, 1), blk(1, 0), blk(2, 2), blk(2, 1), blk(2, 0), bias_spec],
        out_specs=pl.BlockSpec((tq, ATT_WIDTH), lambda b, i: (b * nq + i, 0)),
        out_shape=jax.ShapeDtypeStruct((t, ATT_WIDTH), BF16),
        compiler_params=_params(2),
        name="band_att",
    )(qkv, qkv, qkv, qkv, qkv, qkv, qkv, bias)


def _att_bias(rel_table):
    heads = rel_table.shape[0]
    far = rel_table[:, N_REL - 1:]
    near = rel_table[:, :1]
    n_far = KH_ATT - TH_ATT - REL_MAX
    period = KH_ATT + TH_ATT + 1
    base = jnp.concatenate([
        jnp.broadcast_to(far, (heads, n_far)),
        rel_table[:, ::-1],
        jnp.broadcast_to(near, (heads, KH_ATT - n_far - N_REL)),
        jnp.broadcast_to(far, (heads, period - KH_ATT)),
    ], axis=1)
    rows = jnp.tile(base, (1, TH_ATT))[:, :TH_ATT * (period - 1)].reshape(heads, TH_ATT, period - 1)
    bias = rows[:, :, :KH_ATT].astype(F32) * LOG2_E
    qc = jnp.arange(TH_ATT)[:, None] // CHUNK
    kcol = jnp.arange(KH_ATT)[None, :]
    in_band = (kcol // CHUNK >= qc) & (kcol // CHUNK <= qc + LEFT_CHUNKS)
    n_half = TQ_ATT // TH_ATT
    variants = []
    for tile in range(3):
        for half in range(n_half):
            first_valid = max((2 - tile) * TQ_ATT - half * TH_ATT, 0)
            variants.append(jnp.where((in_band & (kcol >= first_valid))[None], bias, NEG_INF))
    return jnp.stack(variants).reshape(3, n_half, heads, TH_ATT, KH_ATT)


def _memkv_kernel(mem_ref, g_ref, w_ref, kg_ref, g128_ref, mk_ref, mv_ref):
    h = _rms(mem_ref[...], g_ref[0]).astype(BF16)
    kv = _dot(h, w_ref[0])
    mk = kv[:, :MEM_WIDTH]
    ms = _group_sum(mk * mk, g128_ref[...])
    mk_ref[0] = (mk * lax.rsqrt(ms + RMS_EPS) * kg_ref[0]).astype(BF16)
    mv_ref[0] = kv[:, MEM_WIDTH:].astype(BF16)


def _memkv(mem, norm_mem, w_kv, k_gain, g128):
    m, d = mem.shape
    depth = w_kv.shape[0]
    return pl.pallas_call(
        _memkv_kernel,
        grid=(depth,),
        in_specs=[_const_spec((m, d)),
                  pl.BlockSpec((1, 1, d), lambda l: (l, 0, 0)),
                  pl.BlockSpec((1, d, 2 * MEM_WIDTH), lambda l: (l, 0, 0)),
                  pl.BlockSpec((1, 1, MEM_WIDTH), lambda l: (l, 0, 0)),
                  _const_spec(g128.shape)],
        out_specs=[pl.BlockSpec((1, m, MEM_WIDTH), lambda l: (l, 0, 0)),
                   pl.BlockSpec((1, m, MEM_WIDTH), lambda l: (l, 0, 0))],
        out_shape=[jax.ShapeDtypeStruct((depth, m, MEM_WIDTH), BF16),
                   jax.ShapeDtypeStruct((depth, m, MEM_WIDTH), BF16)],
        compiler_params=_params(1),
        name="mem_kv",
    )(mem, norm_mem.reshape(depth, 1, d), w_kv, k_gain.reshape(depth, 1, MEM_WIDTH), g128)


def _memattn_kernel(q_ref, k_ref, v_ref, o_ref):
    heads = [slice(h * MEM_HEAD_DIM, (h + 1) * MEM_HEAD_DIM) for h in range(MEM_HEADS)]
    scores = [_dot_nt(q_ref[:, cols], k_ref[:, cols]) for cols in heads]
    probs = [jnp.exp(s - jnp.max(s, axis=-1, keepdims=True)) for s in scores]
    denoms = [jnp.sum(e, axis=-1, keepdims=True) for e in probs]
    pvs = [_dot(e.astype(BF16), v_ref[:, cols]) for e, cols in zip(probs, heads)]
    for cols, pv, denom in zip(heads, pvs, denoms):
        o_ref[:, cols] = (pv / denom).astype(BF16)


def _memattn(mq, mk, mv, seq):
    t = mq.shape[0]
    tm = TM_MEM
    ns = seq // tm
    m = mk.shape[0] // (t // seq)
    kv_spec = pl.BlockSpec((m, MEM_WIDTH), lambda b, i: (b, 0))
    return pl.pallas_call(
        _memattn_kernel,
        grid=(t // seq, ns),
        in_specs=[pl.BlockSpec((tm, MEM_WIDTH), lambda b, i: (b * ns + i, 0)), kv_spec, kv_spec],
        out_specs=pl.BlockSpec((tm, MEM_WIDTH), lambda b, i: (b * ns + i, 0)),
        out_shape=jax.ShapeDtypeStruct((t, MEM_WIDTH), BF16),
        compiler_params=_params(2),
        name="mem_att",
    )(mq, mk, mv)


def _merge_kernel(x_ref, ng_ref, yr_ref, ya_ref, ym_ref, wg_ref, bg_ref, wr_ref, wa_ref, wm_ref,
                  wo_ref, o_ref):
    d = x_ref.shape[1]

    def tile(rows):
        x = x_ref[rows, :]
        h = _rms(x, ng_ref[...]).astype(BF16)
        gates = jax.nn.sigmoid(_dot(h, wg_ref[...]) + bg_ref[...])
        merged = (gates[:, 0:d] * _dot(yr_ref[rows, :], wr_ref[...])
                  + gates[:, d:2 * d] * _dot(ya_ref[rows, :], wa_ref[...])
                  + gates[:, 2 * d:3 * d] * _dot(ym_ref[rows, :], wm_ref[...]))
        o_ref[rows, :] = x + _dot(merged.astype(BF16), wo_ref[...])

    _for_row_tiles(x_ref.shape[0], TM_MERGE, tile)


def _merge(x, gain, y_rwkv, y_att, y_mem, w_gate, b_gate, w_r, w_a, w_m, w_out, layer):
    t, d = x.shape
    tm = TM_MERGE * SUB_TILES
    tok = lambda width: pl.BlockSpec((tm, width), lambda i: (i, 0))
    return pl.pallas_call(
        _merge_kernel,
        grid=(t // tm,),
        in_specs=[tok(d), _const_spec((1, d)), tok(RWKV_WIDTH), tok(ATT_WIDTH), tok(MEM_WIDTH),
                  _layer_spec(w_gate.shape, layer), _const_spec((1, 3 * d)),
                  _layer_spec(w_r.shape, layer), _layer_spec(w_a.shape, layer),
                  _layer_spec(w_m.shape, layer), _layer_spec(w_out.shape, layer)],
        out_specs=tok(d),
        out_shape=jax.ShapeDtypeStruct((t, d), F32),
        compiler_params=_params(1),
        name="merge",
    )(x, gain.reshape(1, d), y_rwkv, y_att, y_mem, w_gate, b_gate.reshape(1, 3 * d), w_r, w_a, w_m,
      w_out)


def _block_diag_const(width, group, value):
    idx = jnp.arange(width) // group
    return jnp.where(idx[:, None] == idx[None, :], value, 0.0).astype(BF16)


def _chunk_tri(rows):
    i = jnp.arange(rows)
    same_chunk = (i[:, None] // CHUNK) == (i[None, :] // CHUNK)
    return (same_chunk & (i[None, :] <= i[:, None])).astype(BF16)


def kernel(x, mem, norm_ffn1, ffn1_w_in, ffn1_w_out, norm_mix, w_in, shift_mu, decay_w0, decay_lora_b, iclr_a0, iclr_lora_b, gate_lora_b, rwkv_k_k, rwkv_k_a, rwkv_r_k, rwkv_gn_g, rwkv_gn_b, vres_v0, vres_lora_a, vres_lora_b, att_q_norm, att_k_norm, att_rel_bias, norm_mem, mem_w_kv, mem_q_norm, mem_k_norm, w_branch_rwkv, w_branch_att, w_branch_mem, w_gate, b_gate, w_out, norm_ffn2, ffn2_w_in, ffn2_w_out):
    bsz, seq, d = x.shape
    t = bsz * seq
    w_ = RWKV_WIDTH
    depth = w_in.shape[0]
    xf = x.reshape(t, d)

    g64 = _block_diag_const(MXU_DIM, RWKV_HEAD_DIM, 1.0 / RWKV_HEAD_DIM)
    ones64 = _block_diag_const(MXU_DIM, RWKV_HEAD_DIM, 1.0)
    g128 = _block_diag_const(MXU_DIM, MEM_HEAD_DIM, 1.0 / MEM_HEAD_DIM)
    tri = _chunk_tri(MXU_DIM)

    mk_all, mv_all = _memkv(mem.reshape(bsz * mem.shape[1], d), norm_mem, mem_w_kv.astype(BF16),
                            jnp.tile(mem_k_norm, (1, MEM_HEADS)), g128)

    ffn1_in, ffn1_out = ffn1_w_in.astype(BF16), ffn1_w_out.astype(BF16)
    ffn2_in, ffn2_out = ffn2_w_in.astype(BF16), ffn2_w_out.astype(BF16)
    wg, wo = w_gate.astype(BF16), w_out.astype(BF16)
    wbr, wba, wbm = w_branch_rwkv.astype(BF16), w_branch_att.astype(BF16), w_branch_mem.astype(BF16)

    v_first = None
    for l in range(depth):
        xf = _ffn(xf, norm_ffn1[l], ffn1_in, ffn1_out, l)

        use_vres = l > 0
        w_l = w_in[l]
        vres_a = vres_lora_a[l - 1] if use_vres else jnp.zeros((d, VRES_RANK), F32)
        w1 = jnp.concatenate([w_l[:, :RWKV_IN], vres_a, jnp.zeros((d, LANES - VRES_RANK), F32)],
                             axis=1).astype(BF16)
        w2 = w_l[:, RWKV_IN:RWKV_IN + ATT_IN].astype(BF16)
        w3 = w_l[:, RWKV_IN + ATT_IN:].astype(BF16)
        qk_gain = jnp.concatenate([jnp.tile(att_q_norm[l], ATT_HEADS) * (ATT_HEAD_DIM ** -0.5 * LOG2_E),
                                   jnp.tile(att_k_norm[l], ATT_HEADS)]).reshape(1, 2 * ATT_WIDTH)
        mq_gain = (jnp.tile(mem_q_norm[l], MEM_HEADS) * (MEM_HEAD_DIM ** -0.5)).reshape(1, MEM_WIDTH)
        p1, qkv, mq = _proj(xf, norm_mix[l], w1, w2, w3, qk_gain, mq_gain, g64, g128)

        zeros_r = jnp.zeros((DECAY_RANK, w_), F32)
        prm = dict(
            mu=jnp.concatenate([shift_mu[l], jnp.zeros((LANES,), F32)]).reshape(1, RWKV_EXT),
            w0=decay_w0[l].reshape(1, w_), a0=iclr_a0[l].reshape(1, w_),
            lora=jnp.concatenate([jnp.concatenate([decay_lora_b[l], zeros_r], axis=1),
                                  jnp.concatenate([zeros_r, iclr_lora_b[l]], axis=1)], axis=0).astype(BF16),
            gate_b=gate_lora_b[l].astype(BF16),
            k_k=rwkv_k_k[l].reshape(1, w_), k_a=rwkv_k_a[l].reshape(1, w_),
            r_k=rwkv_r_k[l].reshape(1, w_), gn_g=rwkv_gn_g[l].reshape(1, w_),
            gn_b=rwkv_gn_b[l].reshape(1, w_), g64=g64, ones64=ones64, tri=tri)
        if use_vres:
            prm["v0"] = vres_v0[l - 1].reshape(1, w_)
            prm["vres_b"] = jnp.concatenate(
                [vres_lora_b[l - 1], jnp.zeros((LANES - VRES_RANK, w_), F32)], axis=0).astype(BF16)
        y_rwkv, v_first = _rwkv(p1.reshape(bsz, seq, RWKV_EXT), v_first, prm, use_vres)

        y_att = _att(qkv, _att_bias(att_rel_bias[l]), seq)
        y_mem = _memattn(mq, mk_all[l], mv_all[l], seq)

        xf = _merge(xf, norm_mix[l], y_rwkv.reshape(t, w_), y_att, y_mem, wg, b_gate[l], wbr, wba, wbm,
                    wo, l)
        xf = _ffn(xf, norm_ffn2[l], ffn2_in, ffn2_out, l)
    return xf.reshape(bsz, seq, d)
```

```python
import functools
import math

import jax
import jax.numpy as jnp
from jax import lax
from jax.experimental import pallas as pl
from jax.experimental.pallas import tpu as pltpu

F32 = jnp.float32
BF16 = jnp.bfloat16

V7X_VMEM_LIMIT_BYTES = 56 * 1024 * 1024

D_MODEL = 1024
DEPTH = 4
CHUNK = 64
RWKV_HEADS = 8
RWKV_HEAD_DIM = 64
RWKV_WIDTH = RWKV_HEADS * RWKV_HEAD_DIM
DECAY_RANK = 64
ICLR_RANK = 64
GATE_RANK = 128
VRES_RANK = 32
ATT_HEADS = 8
ATT_HEAD_DIM = 64
ATT_WIDTH = ATT_HEADS * ATT_HEAD_DIM
LEFT_CHUNKS = 8
REL_MIN = -(CHUNK - 1)
REL_MAX = 128
N_REL = REL_MAX - REL_MIN + 1
MEM_HEADS = 4
MEM_HEAD_DIM = 128
MEM_WIDTH = MEM_HEADS * MEM_HEAD_DIM
D_FF = 2816
RMS_EPS = 1e-6
GN_EPS = 64e-5
L2_EPS = 1e-12
NEG_INF = -1e30

RWKV_IN = 3 * RWKV_WIDTH + DECAY_RANK + ICLR_RANK + GATE_RANK
LANES = 128
MXU_DIM = 256
RWKV_EXT = RWKV_IN + LANES
ATT_IN = 3 * ATT_WIDTH
HEAD_PAIRS = RWKV_WIDTH // LANES

TM_FFN = 512
TM_PROJ = 512
TM_MERGE = 512
TT_RWKV = 256
TQ_ATT = 512
TH_ATT = 128
KH_ATT = TH_ATT + LEFT_CHUNKS * CHUNK
PAIRS_PER_STAGE = 4
LOG2_E = 1.4426950408889634
TM_MEM = 2048


def _dot(a, b):
    return jnp.dot(a, b, preferred_element_type=F32)


def _dot_nt(a, b):
    return lax.dot_general(a, b, (((1,), (1,)), ((), ())), preferred_element_type=F32)


def _rms(x, gain):
    return x * lax.rsqrt(jnp.mean(x * x, axis=-1, keepdims=True) + RMS_EPS) * gain


def _group_sum(z, g):
    zb = z.astype(BF16)
    return jnp.concatenate([_dot(zb[:, i:i + MXU_DIM], g) for i in range(0, z.shape[1], MXU_DIM)],
                           axis=1)


def _const_spec(shape):
    nd = len(shape)
    return pl.BlockSpec(shape, lambda *_: (0,) * nd, pipeline_mode=pl.Buffered(1))


def _layer_spec(stacked_shape, layer):
    nd = len(stacked_shape) - 1
    return pl.BlockSpec((None,) + tuple(stacked_shape[1:]), lambda *_: (layer,) + (0,) * nd,
                        pipeline_mode=pl.Buffered(1))


def _params(n_axes):
    return pltpu.CompilerParams(dimension_semantics=("arbitrary",) * n_axes,
                                vmem_limit_bytes=V7X_VMEM_LIMIT_BYTES)


def _ffn_kernel(x_ref, g_ref, win_ref, wout_ref, o_ref):
    x = x_ref[...]
    h = _rms(x, g_ref[...]).astype(BF16)
    gu = _dot(h, win_ref[...])
    gate = gu[:, :D_FF]
    up = gu[:, D_FF:]
    act = (gate * jax.nn.sigmoid(gate) * up).astype(BF16)
    o_ref[...] = x + 0.5 * _dot(act, wout_ref[...])


def _ffn(x, gain, w_in, w_out, layer):
    t, d = x.shape
    return pl.pallas_call(
        _ffn_kernel,
        grid=(t // TM_FFN,),
        in_specs=[pl.BlockSpec((TM_FFN, d), lambda i: (i, 0)),
                  _const_spec((1, d)),
                  _layer_spec(w_in.shape, layer),
                  _layer_spec(w_out.shape, layer)],
        out_specs=pl.BlockSpec((TM_FFN, d), lambda i: (i, 0)),
        out_shape=jax.ShapeDtypeStruct((t, d), F32),
        compiler_params=_params(1),
        name="ffn",
    )(x, gain.reshape(1, d), w_in, w_out)


def _proj_kernel(x_ref, g_ref, w1_ref, w2_ref, w3_ref, qkg_ref, mqg_ref, g64_ref, g128_ref,
                 p1_ref, qkv_ref, mq_ref):
    h = _rms(x_ref[...], g_ref[...]).astype(BF16)
    p1_ref[...] = _dot(h, w1_ref[...])
    qkv = _dot(h, w2_ref[...])
    qk = qkv[:, :2 * ATT_WIDTH]
    ms = _group_sum(qk * qk, g64_ref[...])
    qkv_ref[:, :2 * ATT_WIDTH] = (qk * lax.rsqrt(ms + RMS_EPS) * qkg_ref[...]).astype(BF16)
    qkv_ref[:, 2 * ATT_WIDTH:] = qkv[:, 2 * ATT_WIDTH:].astype(BF16)
    mq = _dot(h, w3_ref[...])
    ms = _group_sum(mq * mq, g128_ref[...])
    mq_ref[...] = (mq * lax.rsqrt(ms + RMS_EPS) * mqg_ref[...]).astype(BF16)


def _proj(x, gain, w1, w2, w3, qk_gain, mq_gain, g64, g128):
    t, d = x.shape
    tm = TM_PROJ
    return pl.pallas_call(
        _proj_kernel,
        grid=(t // tm,),
        in_specs=[pl.BlockSpec((tm, d), lambda i: (i, 0)),
                  _const_spec((1, d)),
                  _const_spec(w1.shape), _const_spec(w2.shape), _const_spec(w3.shape),
                  _const_spec(qk_gain.shape), _const_spec(mq_gain.shape),
                  _const_spec(g64.shape), _const_spec(g128.shape)],
        out_specs=[pl.BlockSpec((tm, RWKV_EXT), lambda i: (i, 0)),
                   pl.BlockSpec((tm, ATT_IN), lambda i: (i, 0)),
                   pl.BlockSpec((tm, MEM_WIDTH), lambda i: (i, 0))],
        out_shape=[jax.ShapeDtypeStruct((t, RWKV_EXT), F32),
                   jax.ShapeDtypeStruct((t, ATT_IN), BF16),
                   jax.ShapeDtypeStruct((t, MEM_WIDTH), BF16)],
        compiler_params=_params(1),
        name="proj",
    )(x, gain.reshape(1, d), w1, w2, w3, qk_gain, mq_gain, g64, g128)


def _pair_blockdiag(x, lane_lo):
    zero = jnp.zeros((), x.dtype)
    return jnp.concatenate([jnp.where(lane_lo, x, zero), jnp.where(lane_lo, zero, x)], axis=0)


def _rwkv_kernel(*refs, use_vres, n_batch):
    if use_vres:
        (p_ref, vf_ref, mu_ref, w0_ref, a0_ref, lora_ref, gateb_ref, kk_ref, ka_ref, rk_ref,
         gng_ref, gnb_ref, v0_ref, vresb_ref, g64_ref, ones64_ref, tri_ref,
         y_ref,
         carry_ref, state_ref, rt_ref, at_ref, bt_ref, kt_ref, v_ref, bkt_ref, wcol_ref,
         tat_ref, arb_ref, ul_ref, yl_ref, ys_ref) = refs
        vfo_ref = None
    else:
        (p_ref, mu_ref, w0_ref, a0_ref, lora_ref, gateb_ref, kk_ref, ka_ref, rk_ref,
         gng_ref, gnb_ref, g64_ref, ones64_ref, tri_ref,
         y_ref, vfo_ref,
         carry_ref, state_ref, rt_ref, at_ref, bt_ref, kt_ref, v_ref, bkt_ref, wcol_ref,
         tat_ref, arb_ref, ul_ref, yl_ref, ys_ref) = refs
    tt = p_ref.shape[1]
    rows_all = n_batch * tt
    n_chunks = tt // CHUNK
    n_units = n_batch * n_chunks
    w_ = RWKV_WIDTH

    @pl.when(pl.program_id(0) == 0)
    def _():
        carry_ref[...] = jnp.zeros_like(carry_ref)
        state_ref[...] = jnp.zeros_like(state_ref)

    p = p_ref[...].reshape(rows_all, RWKV_EXT)
    row = lax.broadcasted_iota(jnp.int32, p.shape, 0)
    prev = pltpu.roll(p, 1, axis=0)
    for b in range(n_batch):
        prev = jnp.where(row == b * tt, carry_ref[b:b + 1, :], prev)
    for b in range(n_batch):
        carry_ref[b:b + 1, :] = p[(b + 1) * tt - 1:(b + 1) * tt, :]
    xs = p + (prev - p) * mu_ref[...]

    r = xs[:, 0:w_]
    k = xs[:, w_:2 * w_]
    v = xs[:, 2 * w_:3 * w_]
    lw = xs[:, 3 * w_:3 * w_ + LANES]
    xg = xs[:, 3 * w_ + LANES:3 * w_ + 2 * LANES]

    lane = lax.broadcasted_iota(jnp.int32, lw.shape, 1)
    z = jnp.where(lane < DECAY_RANK, jnp.tanh(lw), lw).astype(BF16)
    lo = _dot(z, lora_ref[...])
    logw = F32(-LOG2_E * math.exp(-0.5)) * jax.nn.sigmoid(w0_ref[...] + lo[:, :w_])
    a = jax.nn.sigmoid(a0_ref[...] + lo[:, w_:])
    g = _dot(jax.nn.sigmoid(xg).astype(BF16), gateb_ref[...])

    if use_vres:
        hv = xs[:, RWKV_IN:RWKV_EXT].astype(BF16)
        mix = jax.nn.sigmoid(v0_ref[...] + _dot(hv, vresb_ref[...]))
        v = v + (vf_ref[...].reshape(rows_all, w_) - v) * mix
    else:
        vfo_ref[...] = v.reshape(n_batch, tt, w_)

    kk = k * kk_ref[...]
    ss = _group_sum(kk * kk, ones64_ref[...])
    kk = kk * lax.rsqrt(jnp.maximum(ss, L2_EPS * L2_EPS))
    k2 = k * (1.0 + (a - 1.0) * ka_ref[...])
    bvec = kk * a

    tri = tri_ref[...]
    hi = logw.astype(BF16)
    mid = (logw - hi.astype(F32)).astype(BF16)
    cum = jnp.concatenate([_dot(tri, hi[i:i + MXU_DIM]) + _dot(tri, mid[i:i + MXU_DIM])
                           for i in range(0, rows_all, MXU_DIM)], axis=0)
    wcum = jnp.exp2(cum)
    inv_w = jnp.exp2(-cum)
    bt = bvec * inv_w
    kt = k2 * inv_w
    rt_ref[...] = (r * wcum).astype(BF16)
    at_ref[...] = (-kk * jnp.exp2(cum - logw)).astype(BF16)
    bt_ref[...] = bt.astype(BF16)
    kt_ref[...] = kt.astype(BF16)
    v_ref[...] = v.astype(BF16)
    for u in range(n_units):
        lo_row, hi_row = u * CHUNK, (u + 1) * CHUNK
        bkt_ref[u] = jnp.concatenate([bt[lo_row:hi_row], kt[lo_row:hi_row]], axis=0).T.astype(BF16)
        wcol_ref[u] = jnp.broadcast_to(wcum[hi_row - 1:hi_row, :], (LANES, w_)).T

    li = lax.broadcasted_iota(jnp.int32, (CHUNK, LANES), 0)
    mi = lax.broadcasted_iota(jnp.int32, (CHUNK, LANES), 1)
    lane_lo = mi < RWKV_HEAD_DIM
    mloc = jnp.where(lane_lo, mi, mi - RWKV_HEAD_DIM)
    strict = mloc < li
    incl = mloc <= li
    eye2 = (mloc == li).astype(F32)
    bi = lax.broadcasted_iota(jnp.int32, (LANES, LANES), 0)
    bj = lax.broadcasted_iota(jnp.int32, (LANES, LANES), 1)
    diag_blocks = (bi < RWKV_HEAD_DIM) == (bj < RWKV_HEAD_DIM)

    def tile_of(unit, pair):
        return (slice(unit * CHUNK, (unit + 1) * CHUNK), slice(pair * LANES, (pair + 1) * LANES))

    tiles = [tile_of(u, pr) for u in range(n_units) for pr in range(HEAD_PAIRS)]
    pw, tinv, grams, rhsl = [], [], [], []
    for rows, cols in tiles:
        at2, rt2, bt2, kt2 = (ref[rows, cols] for ref in (at_ref, rt_ref, bt_ref, kt_ref))
        grams.append(_dot_nt(jnp.concatenate([at2, rt2], axis=0),
                             jnp.concatenate([_pair_blockdiag(bt2, lane_lo), _pair_blockdiag(kt2, lane_lo)],
                                             axis=0)))
    for (rows, cols), gram in zip(tiles, grams):
        v2 = v_ref[rows, cols]
        a_ab = jnp.where(strict, gram[:CHUNK, :LANES], 0.0)
        a_ak = jnp.where(strict, gram[:CHUNK, LANES:], 0.0)
        a_rb = jnp.where(incl, gram[CHUNK:, :LANES], 0.0)
        a_rk = jnp.where(incl, gram[CHUNK:, LANES:], 0.0)
        loc = _dot(jnp.concatenate([a_ak, a_rk], axis=0).astype(BF16), _pair_blockdiag(v2, lane_lo))
        rhsl.append(loc[:CHUNK].astype(BF16))
        yl_ref[rows, cols] = loc[CHUNK:]
        arb_ref[rows, cols] = a_rb.astype(BF16)
        pw.append(a_ab)
        tinv.append(eye2 + a_ab)
    pwb = [x.astype(BF16) for x in pw]
    pw = [_dot(x, _pair_blockdiag(x, lane_lo)) for x in pwb]
    for level in range(5):
        pwb = [x.astype(BF16) for x in pw]
        if level < 4:
            both = [_dot(jnp.concatenate([x, tv.astype(BF16)], axis=0), _pair_blockdiag(x, lane_lo))
                    for x, tv in zip(pwb, tinv)]
            pw = [m[:CHUNK] for m in both]
            tinv = [tv + m[CHUNK:] for tv, m in zip(tinv, both)]
        else:
            tinv = [tv + _dot(tv.astype(BF16), _pair_blockdiag(x, lane_lo))
                    for x, tv in zip(pwb, tinv)]
    for (rows, cols), tv, x in zip(tiles, tinv, rhsl):
        both = _dot(tv.astype(BF16), jnp.concatenate([_pair_blockdiag(at_ref[rows, cols], lane_lo),
                                                      _pair_blockdiag(x, lane_lo)], axis=1))
        tat_ref[rows, cols] = both[:, :LANES].astype(BF16)
        ul_ref[rows, cols] = both[:, LANES:]

    chains = [(b, pr) for b in range(n_batch) for pr in range(HEAD_PAIRS)]
    st = [state_ref[b * HEAD_PAIRS + pr] for b, pr in chains]
    for c in range(n_chunks):
        units = [b * n_chunks + c for b, _ in chains]
        tiles = [tile_of(u, pr) for u, (_, pr) in zip(units, chains)]
        m1 = [_dot(jnp.concatenate([tat_ref[rows, cols], rt_ref[rows, cols]], axis=0), s.astype(BF16))
              for (rows, cols), s in zip(tiles, st)]
        ub = [(m[:CHUNK] + ul_ref[rows, cols]).astype(BF16) for m, (rows, cols) in zip(m1, tiles)]
        new_st = []
        for u, (_, pr), (rows, cols), s, x in zip(units, chains, tiles, st, ub):
            chans = slice(pr * LANES, (pr + 1) * LANES)
            upd = _dot(bkt_ref[u, chans, :], jnp.concatenate([x, v_ref[rows, cols]], axis=0))
            new_st.append((s + jnp.where(diag_blocks, upd, 0.0)) * wcol_ref[u, chans, :])
        st = new_st
        for m, (rows, cols), x in zip(m1, tiles, ub):
            ys_ref[rows, cols] = (m[CHUNK:] + _dot(arb_ref[rows, cols], _pair_blockdiag(x, lane_lo))
                                  + yl_ref[rows, cols])
    for (b, pr), s in zip(chains, st):
        state_ref[b * HEAD_PAIRS + pr] = s

    y = ys_ref[...]
    mean = _group_sum(y, g64_ref[...])
    yc = y - mean
    var = _group_sum(yc * yc, g64_ref[...])
    yn = yc * lax.rsqrt(var + GN_EPS) * gng_ref[...] + gnb_ref[...]
    bonus = _group_sum(r * k2 * rk_ref[...], ones64_ref[...]) * v
    y_ref[...] = ((yn + bonus) * g).astype(BF16).reshape(n_batch, tt, w_)


def _rwkv(p1, v_first, prm, use_vres):
    n_batch, seq, _ = p1.shape
    tt = TT_RWKV
    rows_all = n_batch * tt
    n_units = rows_all // CHUNK
    w_ = RWKV_WIDTH
    tok = lambda width: pl.BlockSpec((n_batch, tt, width), lambda i: (0, i, 0))
    vec = _const_spec((1, w_))
    in_specs = [tok(RWKV_EXT)]
    args = [p1]
    if use_vres:
        in_specs.append(tok(w_))
        args.append(v_first)
    in_specs += [_const_spec((1, RWKV_EXT)), vec, vec, _const_spec((LANES, 2 * w_)),
                 _const_spec((GATE_RANK, w_)), vec, vec, vec, vec, vec]
    args += [prm["mu"], prm["w0"], prm["a0"], prm["lora"], prm["gate_b"], prm["k_k"], prm["k_a"],
             prm["r_k"], prm["gn_g"], prm["gn_b"]]
    if use_vres:
        in_specs += [vec, _const_spec((LANES, w_))]
        args += [prm["v0"], prm["vres_b"]]
    in_specs += [_const_spec((MXU_DIM, MXU_DIM))] * 3
    args += [prm["g64"], prm["ones64"], prm["tri"]]
    out_specs = [tok(w_)]
    out_shape = [jax.ShapeDtypeStruct((n_batch, seq, w_), BF16)]
    if not use_vres:
        out_specs.append(tok(w_))
        out_shape.append(jax.ShapeDtypeStruct((n_batch, seq, w_), F32))
    act_bf16 = pltpu.VMEM((rows_all, w_), BF16)
    act_f32 = pltpu.VMEM((rows_all, w_), F32)
    scratch = [pltpu.VMEM((8, RWKV_EXT), F32),
               pltpu.VMEM((n_batch * HEAD_PAIRS, LANES, LANES), F32),
               act_bf16, act_bf16, act_bf16, act_bf16, act_bf16,
               pltpu.VMEM((n_units, w_, LANES), BF16),
               pltpu.VMEM((n_units, w_, LANES), F32),
               act_bf16, act_bf16, act_f32, act_f32, act_f32]
    out = pl.pallas_call(
        functools.partial(_rwkv_kernel, use_vres=use_vres, n_batch=n_batch),
        grid=(seq // tt,),
        in_specs=in_specs,
        out_specs=out_specs,
        out_shape=out_shape,
        scratch_shapes=scratch,
        compiler_params=_params(1),
        name="rwkv",
    )(*args)
    if use_vres:
        return out[0], v_first
    return out[0], out[1]


def _att_kernel(q_ref, k0_ref, k1_ref, v0_ref, v1_ref, *rest):
    n_half = TQ_ATT // TH_ATT
    bias_refs, o_ref = rest[:n_half], rest[n_half]
    lane = lax.broadcasted_iota(jnp.int32, (1, LANES), 1)
    zero = jnp.zeros((), BF16)
    head_lanes = [(lane >= sub * ATT_HEAD_DIM) & (lane < (sub + 1) * ATT_HEAD_DIM) for sub in range(2)]
    den_lane = [(1 - sub) * ATT_HEAD_DIM for sub in range(2)]
    ones_col = [jnp.where(lane == den_lane[sub], 1.0, 0.0).astype(BF16) for sub in range(2)]
    for group in range(ATT_HEADS // 2 // PAIRS_PER_STAGE):
        units = [(hp, half, sub)
                 for hp in range(group * PAIRS_PER_STAGE, (group + 1) * PAIRS_PER_STAGE)
                 for half in range(n_half) for sub in range(2)]
        k_win, v_win = {}, {}
        for hp in range(group * PAIRS_PER_STAGE, (group + 1) * PAIRS_PER_STAGE):
            cols = slice(hp * LANES, (hp + 1) * LANES)
            k_win[hp] = jnp.concatenate([k0_ref[:, cols], k1_ref[:, cols]], axis=0)
            v_win[hp] = jnp.concatenate([v0_ref[:, cols], v1_ref[:, cols]], axis=0)
        scores = []
        for hp, half, sub in units:
            q2 = q_ref[half * TH_ATT:(half + 1) * TH_ATT, hp * LANES:(hp + 1) * LANES]
            k2 = k_win[hp][half * TH_ATT:half * TH_ATT + KH_ATT]
            scores.append(_dot_nt(jnp.where(head_lanes[sub], q2, zero), k2)
                          + bias_refs[half][0, 2 * hp + sub])
        probs = [jnp.exp2(s - jnp.max(s, axis=-1, keepdims=True)).astype(BF16) for s in scores]
        pvs = []
        for (hp, half, sub), e in zip(units, probs):
            v2 = v_win[hp][half * TH_ATT:half * TH_ATT + KH_ATT]
            pvs.append(_dot(e, jnp.where(head_lanes[sub], v2, zero) + ones_col[sub]))
        for j in range(0, len(units), 2):
            hp, half, _ = units[j]
            lo, hi = pvs[j], pvs[j + 1]
            out = jnp.where(head_lanes[0], lo / lo[:, den_lane[0]:den_lane[0] + 1],
                            hi / hi[:, den_lane[1]:den_lane[1] + 1])
            o_ref[half * TH_ATT:(half + 1) * TH_ATT, hp * LANES:(hp + 1) * LANES] = out.astype(BF16)


def _att(qkv, bias, seq):
    t = qkv.shape[0]
    tq = TQ_ATT
    nq = seq // tq
    blk = lambda col, back: pl.BlockSpec(
        (tq, ATT_WIDTH), lambda b, i: (b * nq + jnp.maximum(i - back, 0), col))
    n_half = tq // TH_ATT
    bias_specs = [pl.BlockSpec((1,) + bias.shape[1:],
                               lambda b, i, half=half: (jnp.where(i == 0, half, n_half), 0, 0, 0))
                  for half in range(n_half)]
    return pl.pallas_call(
        _att_kernel,
        grid=(t // seq, nq),
        in_specs=[blk(0, 0), blk(1, 1), blk(1, 0), blk(2, 1), blk(2, 0)] + bias_specs,
        out_specs=pl.BlockSpec((tq, ATT_WIDTH), lambda b, i: (b * nq + i, 0)),
        out_shape=jax.ShapeDtypeStruct((t, ATT_WIDTH), BF16),
        compiler_params=_params(2),
        name="band_att",
    )(qkv, qkv, qkv, qkv, qkv, *([bias] * n_half))


def _att_bias(rel_table):
    heads = rel_table.shape[0]
    far = rel_table[:, N_REL - 1:]
    near = rel_table[:, :1]
    n_far = KH_ATT - TH_ATT - REL_MAX
    period = KH_ATT + TH_ATT + 1
    base = jnp.concatenate([
        jnp.broadcast_to(far, (heads, n_far)),
        rel_table[:, ::-1],
        jnp.broadcast_to(near, (heads, KH_ATT - n_far - N_REL)),
        jnp.broadcast_to(far, (heads, period - KH_ATT)),
    ], axis=1)
    rows = jnp.tile(base, (1, TH_ATT))[:, :TH_ATT * (period - 1)].reshape(heads, TH_ATT, period - 1)
    bias = rows[:, :, :KH_ATT].astype(F32) * LOG2_E
    qc = jnp.arange(TH_ATT)[:, None] // CHUNK
    kcol = jnp.arange(KH_ATT)[None, :]
    in_band = (kcol // CHUNK >= qc) & (kcol // CHUNK <= qc + LEFT_CHUNKS)
    n_half = TQ_ATT // TH_ATT
    variants = []
    for half in range(n_half):
        first_valid = max(KH_ATT - TH_ATT - half * TH_ATT, 0)
        variants.append(jnp.where((in_band & (kcol >= first_valid))[None], bias, NEG_INF))
    variants.append(jnp.where(in_band[None], bias, NEG_INF))
    return jnp.stack(variants)


def _memkv_kernel(mem_ref, g_ref, w_ref, kg_ref, g128_ref, mk_ref, mv_ref):
    h = _rms(mem_ref[...], g_ref[0]).astype(BF16)
    kv = _dot(h, w_ref[0])
    mk = kv[:, :MEM_WIDTH]
    ms = _group_sum(mk * mk, g128_ref[...])
    mk_ref[0] = (mk * lax.rsqrt(ms + RMS_EPS) * kg_ref[0]).astype(BF16)
    mv_ref[0] = kv[:, MEM_WIDTH:].astype(BF16)


def _memkv(mem, norm_mem, w_kv, k_gain, g128):
    m, d = mem.shape
    depth = w_kv.shape[0]
    return pl.pallas_call(
        _memkv_kernel,
        grid=(depth,),
        in_specs=[_const_spec((m, d)),
                  pl.BlockSpec((1, 1, d), lambda l: (l, 0, 0)),
                  pl.BlockSpec((1, d, 2 * MEM_WIDTH), lambda l: (l, 0, 0)),
                  pl.BlockSpec((1, 1, MEM_WIDTH), lambda l: (l, 0, 0)),
                  _const_spec(g128.shape)],
        out_specs=[pl.BlockSpec((1, m, MEM_WIDTH), lambda l: (l, 0, 0)),
                   pl.BlockSpec((1, m, MEM_WIDTH), lambda l: (l, 0, 0))],
        out_shape=[jax.ShapeDtypeStruct((depth, m, MEM_WIDTH), BF16),
                   jax.ShapeDtypeStruct((depth, m, MEM_WIDTH), BF16)],
        compiler_params=_params(1),
        name="mem_kv",
    )(mem, norm_mem.reshape(depth, 1, d), w_kv, k_gain.reshape(depth, 1, MEM_WIDTH), g128)


def _memattn_kernel(q_ref, k_ref, v_ref, o_ref):
    heads = [slice(h * MEM_HEAD_DIM, (h + 1) * MEM_HEAD_DIM) for h in range(MEM_HEADS)]
    scores = [_dot_nt(q_ref[:, cols], k_ref[:, cols]) for cols in heads]
    probs = [jnp.exp(s - jnp.max(s, axis=-1, keepdims=True)) for s in scores]
    denoms = [jnp.sum(e, axis=-1, keepdims=True) for e in probs]
    pvs = [_dot(e.astype(BF16), v_ref[:, cols]) for e, cols in zip(probs, heads)]
    for cols, pv, denom in zip(heads, pvs, denoms):
        o_ref[:, cols] = (pv / denom).astype(BF16)


def _memattn(mq, mk, mv, seq):
    t = mq.shape[0]
    tm = TM_MEM
    ns = seq // tm
    m = mk.shape[0] // (t // seq)
    kv_spec = pl.BlockSpec((m, MEM_WIDTH), lambda b, i: (b, 0))
    return pl.pallas_call(
        _memattn_kernel,
        grid=(t // seq, ns),
        in_specs=[pl.BlockSpec((tm, MEM_WIDTH), lambda b, i: (b * ns + i, 0)), kv_spec, kv_spec],
        out_specs=pl.BlockSpec((tm, MEM_WIDTH), lambda b, i: (b * ns + i, 0)),
        out_shape=jax.ShapeDtypeStruct((t, MEM_WIDTH), BF16),
        compiler_params=_params(2),
        name="mem_att",
    )(mq, mk, mv)


def _merge_kernel(x_ref, ng_ref, yr_ref, ya_ref, ym_ref, wg_ref, bg_ref, wr_ref, wa_ref, wm_ref,
                  wo_ref, o_ref):
    x = x_ref[...]
    d = x.shape[1]
    h = _rms(x, ng_ref[...]).astype(BF16)
    gates = jax.nn.sigmoid(_dot(h, wg_ref[...]) + bg_ref[...])
    merged = (gates[:, 0:d] * _dot(yr_ref[...], wr_ref[...])
              + gates[:, d:2 * d] * _dot(ya_ref[...], wa_ref[...])
              + gates[:, 2 * d:3 * d] * _dot(ym_ref[...], wm_ref[...]))
    o_ref[...] = x + _dot(merged.astype(BF16), wo_ref[...])


def _merge(x, gain, y_rwkv, y_att, y_mem, w_gate, b_gate, w_r, w_a, w_m, w_out, layer):
    t, d = x.shape
    tm = TM_MERGE
    tok = lambda width: pl.BlockSpec((tm, width), lambda i: (i, 0))
    return pl.pallas_call(
        _merge_kernel,
        grid=(t // tm,),
        in_specs=[tok(d), _const_spec((1, d)), tok(RWKV_WIDTH), tok(ATT_WIDTH), tok(MEM_WIDTH),
                  _layer_spec(w_gate.shape, layer), _const_spec((1, 3 * d)),
                  _layer_spec(w_r.shape, layer), _layer_spec(w_a.shape, layer),
                  _layer_spec(w_m.shape, layer), _layer_spec(w_out.shape, layer)],
        out_specs=tok(d),
        out_shape=jax.ShapeDtypeStruct((t, d), F32),
        compiler_params=_params(1),
        name="merge",
    )(x, gain.reshape(1, d), y_rwkv, y_att, y_mem, w_gate, b_gate.reshape(1, 3 * d), w_r, w_a, w_m,
      w_out)


def _block_diag_const(width, group, value):
    idx = jnp.arange(width) // group
    return jnp.where(idx[:, None] == idx[None, :], value, 0.0).astype(BF16)


def _chunk_tri(rows):
    i = jnp.arange(rows)
    same_chunk = (i[:, None] // CHUNK) == (i[None, :] // CHUNK)
    return (same_chunk & (i[None, :] <= i[:, None])).astype(BF16)


def kernel(x, mem, norm_ffn1, ffn1_w_in, ffn1_w_out, norm_mix, w_in, shift_mu, decay_w0, decay_lora_b, iclr_a0, iclr_lora_b, gate_lora_b, rwkv_k_k, rwkv_k_a, rwkv_r_k, rwkv_gn_g, rwkv_gn_b, vres_v0, vres_lora_a, vres_lora_b, att_q_norm, att_k_norm, att_rel_bias, norm_mem, mem_w_kv, mem_q_norm, mem_k_norm, w_branch_rwkv, w_branch_att, w_branch_mem, w_gate, b_gate, w_out, norm_ffn2, ffn2_w_in, ffn2_w_out):
    bsz, seq, d = x.shape
    assert d == D_MODEL and all(seq % tile == 0 for tile in (TM_FFN, TM_PROJ, TM_MERGE, TT_RWKV, TQ_ATT, TM_MEM))
    t = bsz * seq
    w_ = RWKV_WIDTH
    depth = w_in.shape[0]
    xf = x.reshape(t, d)

    g64 = _block_diag_const(MXU_DIM, RWKV_HEAD_DIM, 1.0 / RWKV_HEAD_DIM)
    ones64 = _block_diag_const(MXU_DIM, RWKV_HEAD_DIM, 1.0)
    g128 = _block_diag_const(MXU_DIM, MEM_HEAD_DIM, 1.0 / MEM_HEAD_DIM)
    tri = _chunk_tri(MXU_DIM)

    mk_all, mv_all = _memkv(mem.reshape(bsz * mem.shape[1], d), norm_mem, mem_w_kv.astype(BF16),
                            jnp.tile(mem_k_norm, (1, MEM_HEADS)), g128)

    ffn1_in, ffn1_out = ffn1_w_in.astype(BF16), ffn1_w_out.astype(BF16)
    ffn2_in, ffn2_out = ffn2_w_in.astype(BF16), ffn2_w_out.astype(BF16)
    wg, wo = w_gate.astype(BF16), w_out.astype(BF16)
    wbr, wba, wbm = w_branch_rwkv.astype(BF16), w_branch_att.astype(BF16), w_branch_mem.astype(BF16)

    v_first = None
    for l in range(depth):
        xf = _ffn(xf, norm_ffn1[l], ffn1_in, ffn1_out, l)

        use_vres = l > 0
        w_l = w_in[l]
        vres_a = vres_lora_a[l - 1] if use_vres else jnp.zeros((d, VRES_RANK), F32)
        w1 = jnp.concatenate([w_l[:, :RWKV_IN], vres_a, jnp.zeros((d, LANES - VRES_RANK), F32)],
                             axis=1).astype(BF16)
        w2 = w_l[:, RWKV_IN:RWKV_IN + ATT_IN].astype(BF16)
        w3 = w_l[:, RWKV_IN + ATT_IN:].astype(BF16)
        qk_gain = jnp.concatenate([jnp.tile(att_q_norm[l], ATT_HEADS) * (ATT_HEAD_DIM ** -0.5 * LOG2_E),
                                   jnp.tile(att_k_norm[l], ATT_HEADS)]).reshape(1, 2 * ATT_WIDTH)
        mq_gain = (jnp.tile(mem_q_norm[l], MEM_HEADS) * (MEM_HEAD_DIM ** -0.5)).reshape(1, MEM_WIDTH)
        p1, qkv, mq = _proj(xf, norm_mix[l], w1, w2, w3, qk_gain, mq_gain, g64, g128)

        zeros_r = jnp.zeros((DECAY_RANK, w_), F32)
        prm = dict(
            mu=jnp.concatenate([shift_mu[l], jnp.zeros((LANES,), F32)]).reshape(1, RWKV_EXT),
            w0=decay_w0[l].reshape(1, w_), a0=iclr_a0[l].reshape(1, w_),
            lora=jnp.concatenate([jnp.concatenate([decay_lora_b[l], zeros_r], axis=1),
                                  jnp.concatenate([zeros_r, iclr_lora_b[l]], axis=1)], axis=0).astype(BF16),
            gate_b=gate_lora_b[l].astype(BF16),
            k_k=rwkv_k_k[l].reshape(1, w_), k_a=rwkv_k_a[l].reshape(1, w_),
            r_k=rwkv_r_k[l].reshape(1, w_), gn_g=rwkv_gn_g[l].reshape(1, w_),
            gn_b=rwkv_gn_b[l].reshape(1, w_), g64=g64, ones64=ones64, tri=tri)
        if use_vres:
            prm["v0"] = vres_v0[l - 1].reshape(1, w_)
            prm["vres_b"] = jnp.concatenate(
                [vres_lora_b[l - 1], jnp.zeros((LANES - VRES_RANK, w_), F32)], axis=0).astype(BF16)
        y_rwkv, v_first = _rwkv(p1.reshape(bsz, seq, RWKV_EXT), v_first, prm, use_vres)

        y_att = _att(qkv, _att_bias(att_rel_bias[l]), seq)
        y_mem = _memattn(mq, mk_all[l], mv_all[l], seq)

        xf = _merge(xf, norm_mix[l], y_rwkv.reshape(t, w_), y_att, y_mem, wg, b_gate[l], wbr, wba, wbm,
                    wo, l)
        xf = _ffn(xf, norm_ffn2[l], ffn2_in, ffn2_out, l)
    return xf.reshape(bsz, seq, d)
```

```python
import functools
import math

import jax
import jax.numpy as jnp
from jax import lax
from jax.experimental import pallas as pl
from jax.experimental.pallas import tpu as pltpu

F32 = jnp.float32
BF16 = jnp.bfloat16

V7X_VMEM_LIMIT_BYTES = 56 * 1024 * 1024

D_MODEL = 1024
DEPTH = 4
CHUNK = 64
RWKV_HEADS = 8
RWKV_HEAD_DIM = 64
RWKV_WIDTH = RWKV_HEADS * RWKV_HEAD_DIM
DECAY_RANK = 64
ICLR_RANK = 64
GATE_RANK = 128
VRES_RANK = 32
ATT_HEADS = 8
ATT_HEAD_DIM = 64
ATT_WIDTH = ATT_HEADS * ATT_HEAD_DIM
LEFT_CHUNKS = 8
REL_MIN = -(CHUNK - 1)
REL_MAX = 128
N_REL = REL_MAX - REL_MIN + 1
MEM_HEADS = 4
MEM_HEAD_DIM = 128
MEM_WIDTH = MEM_HEADS * MEM_HEAD_DIM
D_FF = 2816
RMS_EPS = 1e-6
GN_EPS = 64e-5
L2_EPS = 1e-12
NEG_INF = -1e30

RWKV_IN = 3 * RWKV_WIDTH + DECAY_RANK + ICLR_RANK + GATE_RANK
LANES = 128
MXU_DIM = 256
RWKV_EXT = RWKV_IN + LANES
ATT_IN = 3 * ATT_WIDTH
HEAD_PAIRS = RWKV_WIDTH // LANES

TM_FFN = 512
TM_PROJ = 512
TM_MERGE = 512
TT_RWKV = 256
TQ_ATT = 512
TH_ATT = 128
KH_ATT = TH_ATT + LEFT_CHUNKS * CHUNK
PAIRS_PER_STAGE = 4
LOG2_E = 1.4426950408889634
TM_MEM = 2048


def _dot(a, b):
    return jnp.dot(a, b, preferred_element_type=F32)


def _dot_nt(a, b):
    return lax.dot_general(a, b, (((1,), (1,)), ((), ())), preferred_element_type=F32)


def _rms(x, gain):
    return x * lax.rsqrt(jnp.mean(x * x, axis=-1, keepdims=True) + RMS_EPS) * gain


def _group_sum(z, g):
    zb = z.astype(BF16)
    return jnp.concatenate([_dot(zb[:, i:i + MXU_DIM], g) for i in range(0, z.shape[1], MXU_DIM)],
                           axis=1)


def _const_spec(shape):
    nd = len(shape)
    return pl.BlockSpec(shape, lambda *_: (0,) * nd, pipeline_mode=pl.Buffered(1))


def _layer_spec(stacked_shape, layer):
    nd = len(stacked_shape) - 1
    return pl.BlockSpec((None,) + tuple(stacked_shape[1:]), lambda *_: (layer,) + (0,) * nd,
                        pipeline_mode=pl.Buffered(1))


def _params(n_axes):
    return pltpu.CompilerParams(dimension_semantics=("arbitrary",) * n_axes,
                                vmem_limit_bytes=V7X_VMEM_LIMIT_BYTES)


def _ffn_kernel(x_ref, g_ref, win_ref, wout_ref, o_ref):
    x = x_ref[...]
    h = _rms(x, g_ref[...]).astype(BF16)
    gu = _dot(h, win_ref[...])
    gate = gu[:, :D_FF]
    up = gu[:, D_FF:]
    act = (gate * jax.nn.sigmoid(gate) * up).astype(BF16)
    o_ref[...] = x + 0.5 * _dot(act, wout_ref[...])


def _ffn(x, gain, w_in, w_out, layer):
    t, d = x.shape
    return pl.pallas_call(
        _ffn_kernel,
        grid=(t // TM_FFN,),
        in_specs=[pl.BlockSpec((TM_FFN, d), lambda i: (i, 0)),
                  _const_spec((1, d)),
                  _layer_spec(w_in.shape, layer),
                  _layer_spec(w_out.shape, layer)],
        out_specs=pl.BlockSpec((TM_FFN, d), lambda i: (i, 0)),
        out_shape=jax.ShapeDtypeStruct((t, d), F32),
        compiler_params=_params(1),
        name="ffn",
    )(x, gain.reshape(1, d), w_in, w_out)


def _proj_kernel(x_ref, g_ref, w_ref, vres_ref, qkg_ref, mqg_ref, g64_ref, g128_ref,
                 p1_ref, qkv_ref, mq_ref):
    h = _rms(x_ref[...], g_ref[...]).astype(BF16)
    p1_ref[:, :RWKV_IN] = _dot(h, w_ref[:, :RWKV_IN])
    p1_ref[:, RWKV_IN:] = _dot(h, vres_ref[...])
    qkv = _dot(h, w_ref[:, RWKV_IN:RWKV_IN + ATT_IN])
    qk = qkv[:, :2 * ATT_WIDTH]
    ms = _group_sum(qk * qk, g64_ref[...])
    qkv_ref[:, :2 * ATT_WIDTH] = (qk * lax.rsqrt(ms + RMS_EPS) * qkg_ref[...]).astype(BF16)
    qkv_ref[:, 2 * ATT_WIDTH:] = qkv[:, 2 * ATT_WIDTH:].astype(BF16)
    mq = _dot(h, w_ref[:, RWKV_IN + ATT_IN:])
    ms = _group_sum(mq * mq, g128_ref[...])
    mq_ref[...] = (mq * lax.rsqrt(ms + RMS_EPS) * mqg_ref[...]).astype(BF16)


def _proj(x, gain, w, vres_a, qk_gain, mq_gain, g64, g128, layer):
    t, d = x.shape
    tm = TM_PROJ
    return pl.pallas_call(
        _proj_kernel,
        grid=(t // tm,),
        in_specs=[pl.BlockSpec((tm, d), lambda i: (i, 0)),
                  _const_spec((1, d)),
                  _layer_spec(w.shape, layer), _const_spec(vres_a.shape),
                  _const_spec(qk_gain.shape), _const_spec(mq_gain.shape),
                  _const_spec(g64.shape), _const_spec(g128.shape)],
        out_specs=[pl.BlockSpec((tm, RWKV_EXT), lambda i: (i, 0)),
                   pl.BlockSpec((tm, ATT_IN), lambda i: (i, 0)),
                   pl.BlockSpec((tm, MEM_WIDTH), lambda i: (i, 0))],
        out_shape=[jax.ShapeDtypeStruct((t, RWKV_EXT), F32),
                   jax.ShapeDtypeStruct((t, ATT_IN), BF16),
                   jax.ShapeDtypeStruct((t, MEM_WIDTH), BF16)],
        compiler_params=_params(1),
        name="proj",
    )(x, gain.reshape(1, d), w, vres_a, qk_gain, mq_gain, g64, g128)


def _pair_blockdiag(x, lane_lo):
    zero = jnp.zeros((), x.dtype)
    return jnp.concatenate([jnp.where(lane_lo, x, zero), jnp.where(lane_lo, zero, x)], axis=0)


def _rwkv_kernel(*refs, use_vres, n_batch):
    if use_vres:
        (p_ref, vf_ref, mu_ref, w0_ref, a0_ref, lora_ref, gateb_ref, kk_ref, ka_ref, rk_ref,
         gng_ref, gnb_ref, v0_ref, vresb_ref, g64_ref, ones64_ref, tri_ref,
         y_ref,
         carry_ref, state_ref, rt_ref, at_ref, bt_ref, kt_ref, v_ref, bkt_ref, wcol_ref,
         tat_ref, arb_ref, ul_ref, yl_ref, ys_ref) = refs
        vfo_ref = None
    else:
        (p_ref, mu_ref, w0_ref, a0_ref, lora_ref, gateb_ref, kk_ref, ka_ref, rk_ref,
         gng_ref, gnb_ref, g64_ref, ones64_ref, tri_ref,
         y_ref, vfo_ref,
         carry_ref, state_ref, rt_ref, at_ref, bt_ref, kt_ref, v_ref, bkt_ref, wcol_ref,
         tat_ref, arb_ref, ul_ref, yl_ref, ys_ref) = refs
    tt = p_ref.shape[1]
    rows_all = n_batch * tt
    n_chunks = tt // CHUNK
    n_units = n_batch * n_chunks
    w_ = RWKV_WIDTH

    @pl.when(pl.program_id(0) == 0)
    def _():
        carry_ref[...] = jnp.zeros_like(carry_ref)
        state_ref[...] = jnp.zeros_like(state_ref)

    p = p_ref[...].reshape(rows_all, RWKV_EXT)
    row = lax.broadcasted_iota(jnp.int32, p.shape, 0)
    prev = pltpu.roll(p, 1, axis=0)
    for b in range(n_batch):
        prev = jnp.where(row == b * tt, carry_ref[b:b + 1, :], prev)
    for b in range(n_batch):
        carry_ref[b:b + 1, :] = p[(b + 1) * tt - 1:(b + 1) * tt, :]
    xs = p + (prev - p) * mu_ref[...]

    r = xs[:, 0:w_]
    k = xs[:, w_:2 * w_]
    v = xs[:, 2 * w_:3 * w_]
    lw = xs[:, 3 * w_:3 * w_ + LANES]
    xg = xs[:, 3 * w_ + LANES:3 * w_ + 2 * LANES]

    lane = lax.broadcasted_iota(jnp.int32, lw.shape, 1)
    z = jnp.where(lane < DECAY_RANK, jnp.tanh(lw), lw).astype(BF16)
    lo = _dot(z, lora_ref[...])
    logw = F32(-LOG2_E * math.exp(-0.5)) * jax.nn.sigmoid(w0_ref[...] + lo[:, :w_])
    a = jax.nn.sigmoid(a0_ref[...] + lo[:, w_:])
    g = _dot(jax.nn.sigmoid(xg).astype(BF16), gateb_ref[...])

    if use_vres:
        hv = xs[:, RWKV_IN:RWKV_EXT].astype(BF16)
        mix = jax.nn.sigmoid(v0_ref[...] + _dot(hv, vresb_ref[...]))
        v = v + (vf_ref[...].reshape(rows_all, w_) - v) * mix
    else:
        vfo_ref[...] = v.reshape(n_batch, tt, w_)

    kk = k * kk_ref[...]
    ss = _group_sum(kk * kk, ones64_ref[...])
    kk = kk * lax.rsqrt(jnp.maximum(ss, L2_EPS * L2_EPS))
    k2 = k * (1.0 + (a - 1.0) * ka_ref[...])
    bvec = kk * a

    tri = tri_ref[...]
    hi = logw.astype(BF16)
    mid = (logw - hi.astype(F32)).astype(BF16)
    cum = jnp.concatenate([_dot(tri, hi[i:i + MXU_DIM]) + _dot(tri, mid[i:i + MXU_DIM])
                           for i in range(0, rows_all, MXU_DIM)], axis=0)
    wcum = jnp.exp2(cum)
    inv_w = jnp.exp2(-cum)
    bt = bvec * inv_w
    kt = k2 * inv_w
    rt_ref[...] = (r * wcum).astype(BF16)
    at_ref[...] = (-kk * jnp.exp2(cum - logw)).astype(BF16)
    bt_ref[...] = bt.astype(BF16)
    kt_ref[...] = kt.astype(BF16)
    v_ref[...] = v.astype(BF16)
    for u in range(n_units):
        lo_row, hi_row = u * CHUNK, (u + 1) * CHUNK
        bkt_ref[u] = jnp.concatenate([bt[lo_row:hi_row], kt[lo_row:hi_row]], axis=0).T.astype(BF16)
        wcol_ref[u] = jnp.broadcast_to(wcum[hi_row - 1:hi_row, :], (LANES, w_)).T

    li = lax.broadcasted_iota(jnp.int32, (CHUNK, LANES), 0)
    mi = lax.broadcasted_iota(jnp.int32, (CHUNK, LANES), 1)
    lane_lo = mi < RWKV_HEAD_DIM
    mloc = jnp.where(lane_lo, mi, mi - RWKV_HEAD_DIM)
    strict = mloc < li
    incl = mloc <= li
    eye2 = (mloc == li).astype(F32)
    bi = lax.broadcasted_iota(jnp.int32, (LANES, LANES), 0)
    bj = lax.broadcasted_iota(jnp.int32, (LANES, LANES), 1)
    diag_blocks = (bi < RWKV_HEAD_DIM) == (bj < RWKV_HEAD_DIM)

    def tile_of(unit, pair):
        return (slice(unit * CHUNK, (unit + 1) * CHUNK), slice(pair * LANES, (pair + 1) * LANES))

    tiles = [tile_of(u, pr) for u in range(n_units) for pr in range(HEAD_PAIRS)]
    pw, tinv, grams, rhsl = [], [], [], []
    for rows, cols in tiles:
        at2, rt2, bt2, kt2 = (ref[rows, cols] for ref in (at_ref, rt_ref, bt_ref, kt_ref))
        grams.append(_dot_nt(jnp.concatenate([at2, rt2], axis=0),
                             jnp.concatenate([_pair_blockdiag(bt2, lane_lo), _pair_blockdiag(kt2, lane_lo)],
                                             axis=0)))
    for (rows, cols), gram in zip(tiles, grams):
        v2 = v_ref[rows, cols]
        a_ab = jnp.where(strict, gram[:CHUNK, :LANES], 0.0)
        a_ak = jnp.where(strict, gram[:CHUNK, LANES:], 0.0)
        a_rb = jnp.where(incl, gram[CHUNK:, :LANES], 0.0)
        a_rk = jnp.where(incl, gram[CHUNK:, LANES:], 0.0)
        loc = _dot(jnp.concatenate([a_ak, a_rk], axis=0).astype(BF16), _pair_blockdiag(v2, lane_lo))
        rhsl.append(loc[:CHUNK].astype(BF16))
        yl_ref[rows, cols] = loc[CHUNK:]
        arb_ref[rows, cols] = a_rb.astype(BF16)
        pw.append(a_ab)
        tinv.append(eye2 + a_ab)
    pwb = [x.astype(BF16) for x in pw]
    pw = [_dot(x, _pair_blockdiag(x, lane_lo)) for x in pwb]
    for level in range(5):
        pwb = [x.astype(BF16) for x in pw]
        if level < 4:
            both = [_dot(jnp.concatenate([x, tv.astype(BF16)], axis=0), _pair_blockdiag(x, lane_lo))
                    for x, tv in zip(pwb, tinv)]
            pw = [m[:CHUNK] for m in both]
            tinv = [tv + m[CHUNK:] for tv, m in zip(tinv, both)]
        else:
            tinv = [tv + _dot(tv.astype(BF16), _pair_blockdiag(x, lane_lo))
                    for x, tv in zip(pwb, tinv)]
    for (rows, cols), tv, x in zip(tiles, tinv, rhsl):
        both = _dot(tv.astype(BF16), jnp.concatenate([_pair_blockdiag(at_ref[rows, cols], lane_lo),
                                                      _pair_blockdiag(x, lane_lo)], axis=1))
        tat_ref[rows, cols] = both[:, :LANES].astype(BF16)
        ul_ref[rows, cols] = both[:, LANES:]

    chains = [(b, pr) for b in range(n_batch) for pr in range(HEAD_PAIRS)]
    st = [state_ref[b * HEAD_PAIRS + pr] for b, pr in chains]
    for c in range(n_chunks):
        units = [b * n_chunks + c for b, _ in chains]
        tiles = [tile_of(u, pr) for u, (_, pr) in zip(units, chains)]
        m1 = [_dot(jnp.concatenate([tat_ref[rows, cols], rt_ref[rows, cols]], axis=0), s.astype(BF16))
              for (rows, cols), s in zip(tiles, st)]
        ub = [(m[:CHUNK] + ul_ref[rows, cols]).astype(BF16) for m, (rows, cols) in zip(m1, tiles)]
        new_st = []
        for u, (_, pr), (rows, cols), s, x in zip(units, chains, tiles, st, ub):
            chans = slice(pr * LANES, (pr + 1) * LANES)
            upd = _dot(bkt_ref[u, chans, :], jnp.concatenate([x, v_ref[rows, cols]], axis=0))
            new_st.append((s + jnp.where(diag_blocks, upd, 0.0)) * wcol_ref[u, chans, :])
        st = new_st
        for m, (rows, cols), x in zip(m1, tiles, ub):
            ys_ref[rows, cols] = (m[CHUNK:] + _dot(arb_ref[rows, cols], _pair_blockdiag(x, lane_lo))
                                  + yl_ref[rows, cols])
    for (b, pr), s in zip(chains, st):
        state_ref[b * HEAD_PAIRS + pr] = s

    y = ys_ref[...]
    mean = _group_sum(y, g64_ref[...])
    yc = y - mean
    var = _group_sum(yc * yc, g64_ref[...])
    yn = yc * lax.rsqrt(var + GN_EPS) * gng_ref[...] + gnb_ref[...]
    bonus = _group_sum(r * k2 * rk_ref[...], ones64_ref[...]) * v
    y_ref[...] = ((yn + bonus) * g).astype(BF16).reshape(n_batch, tt, w_)


def _rwkv(p1, v_first, prm, use_vres):
    n_batch, seq, _ = p1.shape
    tt = TT_RWKV
    rows_all = n_batch * tt
    n_units = rows_all // CHUNK
    w_ = RWKV_WIDTH
    tok = lambda width: pl.BlockSpec((n_batch, tt, width), lambda i: (0, i, 0))
    vec = _const_spec((1, w_))
    in_specs = [tok(RWKV_EXT)]
    args = [p1]
    if use_vres:
        in_specs.append(tok(w_))
        args.append(v_first)
    in_specs += [_const_spec((1, RWKV_EXT)), vec, vec, _const_spec((LANES, 2 * w_)),
                 _const_spec((GATE_RANK, w_)), vec, vec, vec, vec, vec]
    args += [prm["mu"], prm["w0"], prm["a0"], prm["lora"], prm["gate_b"], prm["k_k"], prm["k_a"],
             prm["r_k"], prm["gn_g"], prm["gn_b"]]
    if use_vres:
        in_specs += [vec, _const_spec((LANES, w_))]
        args += [prm["v0"], prm["vres_b"]]
    in_specs += [_const_spec((MXU_DIM, MXU_DIM))] * 3
    args += [prm["g64"], prm["ones64"], prm["tri"]]
    out_specs = [tok(w_)]
    out_shape = [jax.ShapeDtypeStruct((n_batch, seq, w_), BF16)]
    if not use_vres:
        out_specs.append(tok(w_))
        out_shape.append(jax.ShapeDtypeStruct((n_batch, seq, w_), F32))
    act_bf16 = pltpu.VMEM((rows_all, w_), BF16)
    act_f32 = pltpu.VMEM((rows_all, w_), F32)
    scratch = [pltpu.VMEM((8, RWKV_EXT), F32),
               pltpu.VMEM((n_batch * HEAD_PAIRS, LANES, LANES), F32),
               act_bf16, act_bf16, act_bf16, act_bf16, act_bf16,
               pltpu.VMEM((n_units, w_, LANES), BF16),
               pltpu.VMEM((n_units, w_, LANES), F32),
               act_bf16, act_bf16, act_f32, act_f32, act_f32]
    out = pl.pallas_call(
        functools.partial(_rwkv_kernel, use_vres=use_vres, n_batch=n_batch),
        grid=(seq // tt,),
        in_specs=in_specs,
        out_specs=out_specs,
        out_shape=out_shape,
        scratch_shapes=scratch,
        compiler_params=_params(1),
        name="rwkv",
    )(*args)
    if use_vres:
        return out[0], v_first
    return out[0], out[1]


def _att_kernel(q_ref, k0_ref, k1_ref, v0_ref, v1_ref, *rest):
    n_half = TQ_ATT // TH_ATT
    bias_refs, o_ref = rest[:n_half], rest[n_half]
    lane = lax.broadcasted_iota(jnp.int32, (1, LANES), 1)
    zero = jnp.zeros((), BF16)
    head_lanes = [(lane >= sub * ATT_HEAD_DIM) & (lane < (sub + 1) * ATT_HEAD_DIM) for sub in range(2)]
    den_lane = [(1 - sub) * ATT_HEAD_DIM for sub in range(2)]
    ones_col = [jnp.where(lane == den_lane[sub], 1.0, 0.0).astype(BF16) for sub in range(2)]
    for group in range(ATT_HEADS // 2 // PAIRS_PER_STAGE):
        units = [(hp, half, sub)
                 for hp in range(group * PAIRS_PER_STAGE, (group + 1) * PAIRS_PER_STAGE)
                 for half in range(n_half) for sub in range(2)]
        k_win, v_win = {}, {}
        for hp in range(group * PAIRS_PER_STAGE, (group + 1) * PAIRS_PER_STAGE):
            cols = slice(hp * LANES, (hp + 1) * LANES)
            k_win[hp] = jnp.concatenate([k0_ref[:, cols], k1_ref[:, cols]], axis=0)
            v_win[hp] = jnp.concatenate([v0_ref[:, cols], v1_ref[:, cols]], axis=0)
        scores = []
        for hp, half, sub in units:
            q2 = q_ref[half * TH_ATT:(half + 1) * TH_ATT, hp * LANES:(hp + 1) * LANES]
            k2 = k_win[hp][half * TH_ATT:half * TH_ATT + KH_ATT]
            scores.append(_dot_nt(jnp.where(head_lanes[sub], q2, zero), k2)
                          + bias_refs[half][0, 2 * hp + sub])
        probs = [jnp.exp2(s - jnp.max(s, axis=-1, keepdims=True)).astype(BF16) for s in scores]
        pvs = []
        for (hp, half, sub), e in zip(units, probs):
            v2 = v_win[hp][half * TH_ATT:half * TH_ATT + KH_ATT]
            pvs.append(_dot(e, jnp.where(head_lanes[sub], v2, zero) + ones_col[sub]))
        for j in range(0, len(units), 2):
            hp, half, _ = units[j]
            lo, hi = pvs[j], pvs[j + 1]
            out = jnp.where(head_lanes[0], lo / lo[:, den_lane[0]:den_lane[0] + 1],
                            hi / hi[:, den_lane[1]:den_lane[1] + 1])
            o_ref[half * TH_ATT:(half + 1) * TH_ATT, hp * LANES:(hp + 1) * LANES] = out.astype(BF16)


def _att(qkv, bias, seq):
    t = qkv.shape[0]
    tq = TQ_ATT
    nq = seq // tq
    blk = lambda col, back: pl.BlockSpec(
        (tq, ATT_WIDTH), lambda b, i: (b * nq + jnp.maximum(i - back, 0), col))
    n_half = tq // TH_ATT
    bias_specs = [pl.BlockSpec((1,) + bias.shape[1:],
                               lambda b, i, half=half: (jnp.where(i == 0, half, n_half), 0, 0, 0))
                  for half in range(n_half)]
    return pl.pallas_call(
        _att_kernel,
        grid=(t // seq, nq),
        in_specs=[blk(0, 0), blk(1, 1), blk(1, 0), blk(2, 1), blk(2, 0)] + bias_specs,
        out_specs=pl.BlockSpec((tq, ATT_WIDTH), lambda b, i: (b * nq + i, 0)),
        out_shape=jax.ShapeDtypeStruct((t, ATT_WIDTH), BF16),
        compiler_params=_params(2),
        name="band_att",
    )(qkv, qkv, qkv, qkv, qkv, *([bias] * n_half))


def _att_bias(rel_table):
    heads = rel_table.shape[0]
    far = rel_table[:, N_REL - 1:]
    near = rel_table[:, :1]
    n_far = KH_ATT - TH_ATT - REL_MAX
    period = KH_ATT + TH_ATT + 1
    base = jnp.concatenate([
        jnp.broadcast_to(far, (heads, n_far)),
        rel_table[:, ::-1],
        jnp.broadcast_to(near, (heads, KH_ATT - n_far - N_REL)),
        jnp.broadcast_to(far, (heads, period - KH_ATT)),
    ], axis=1)
    rows = jnp.tile(base, (1, TH_ATT))[:, :TH_ATT * (period - 1)].reshape(heads, TH_ATT, period - 1)
    bias = rows[:, :, :KH_ATT].astype(F32) * LOG2_E
    qc = jnp.arange(TH_ATT)[:, None] // CHUNK
    kcol = jnp.arange(KH_ATT)[None, :]
    in_band = (kcol // CHUNK >= qc) & (kcol // CHUNK <= qc + LEFT_CHUNKS)
    n_half = TQ_ATT // TH_ATT
    variants = []
    for half in range(n_half):
        first_valid = max(KH_ATT - TH_ATT - half * TH_ATT, 0)
        variants.append(jnp.where((in_band & (kcol >= first_valid))[None], bias, NEG_INF))
    variants.append(jnp.where(in_band[None], bias, NEG_INF))
    return jnp.stack(variants)


def _memkv_kernel(mem_ref, g_ref, w_ref, kg_ref, g128_ref, mk_ref, mv_ref):
    h = _rms(mem_ref[...], g_ref[0]).astype(BF16)
    kv = _dot(h, w_ref[0])
    mk = kv[:, :MEM_WIDTH]
    ms = _group_sum(mk * mk, g128_ref[...])
    mk_ref[0] = (mk * lax.rsqrt(ms + RMS_EPS) * kg_ref[0]).astype(BF16)
    mv_ref[0] = kv[:, MEM_WIDTH:].astype(BF16)


def _memkv(mem, norm_mem, w_kv, k_gain, g128):
    m, d = mem.shape
    depth = w_kv.shape[0]
    return pl.pallas_call(
        _memkv_kernel,
        grid=(depth,),
        in_specs=[_const_spec((m, d)),
                  pl.BlockSpec((1, 1, d), lambda l: (l, 0, 0)),
                  pl.BlockSpec((1, d, 2 * MEM_WIDTH), lambda l: (l, 0, 0)),
                  pl.BlockSpec((1, 1, MEM_WIDTH), lambda l: (l, 0, 0)),
                  _const_spec(g128.shape)],
        out_specs=[pl.BlockSpec((1, m, MEM_WIDTH), lambda l: (l, 0, 0)),
                   pl.BlockSpec((1, m, MEM_WIDTH), lambda l: (l, 0, 0))],
        out_shape=[jax.ShapeDtypeStruct((depth, m, MEM_WIDTH), BF16),
                   jax.ShapeDtypeStruct((depth, m, MEM_WIDTH), BF16)],
        compiler_params=_params(1),
        name="mem_kv",
    )(mem, norm_mem.reshape(depth, 1, d), w_kv, k_gain.reshape(depth, 1, MEM_WIDTH), g128)


def _memattn_kernel(q_ref, k_ref, v_ref, o_ref):
    heads = [slice(h * MEM_HEAD_DIM, (h + 1) * MEM_HEAD_DIM) for h in range(MEM_HEADS)]
    scores = [_dot_nt(q_ref[:, cols], k_ref[:, cols]) for cols in heads]
    probs = [jnp.exp(s - jnp.max(s, axis=-1, keepdims=True)) for s in scores]
    denoms = [jnp.sum(e, axis=-1, keepdims=True) for e in probs]
    pvs = [_dot(e.astype(BF16), v_ref[:, cols]) for e, cols in zip(probs, heads)]
    for cols, pv, denom in zip(heads, pvs, denoms):
        o_ref[:, cols] = (pv / denom).astype(BF16)


def _memattn(mq, mk, mv, seq):
    t = mq.shape[0]
    tm = TM_MEM
    ns = seq // tm
    m = mk.shape[0] // (t // seq)
    kv_spec = pl.BlockSpec((m, MEM_WIDTH), lambda b, i: (b, 0))
    return pl.pallas_call(
        _memattn_kernel,
        grid=(t // seq, ns),
        in_specs=[pl.BlockSpec((tm, MEM_WIDTH), lambda b, i: (b * ns + i, 0)), kv_spec, kv_spec],
        out_specs=pl.BlockSpec((tm, MEM_WIDTH), lambda b, i: (b * ns + i, 0)),
        out_shape=jax.ShapeDtypeStruct((t, MEM_WIDTH), BF16),
        compiler_params=_params(2),
        name="mem_att",
    )(mq, mk, mv)


def _merge_kernel(x_ref, ng_ref, yr_ref, ya_ref, ym_ref, wg_ref, bg_ref, wr_ref, wa_ref, wm_ref,
                  wo_ref, o_ref):
    x = x_ref[...]
    d = x.shape[1]
    h = _rms(x, ng_ref[...]).astype(BF16)
    gates = jax.nn.sigmoid(_dot(h, wg_ref[...]) + bg_ref[...])
    merged = (gates[:, 0:d] * _dot(yr_ref[...], wr_ref[...])
              + gates[:, d:2 * d] * _dot(ya_ref[...], wa_ref[...])
              + gates[:, 2 * d:3 * d] * _dot(ym_ref[...], wm_ref[...]))
    o_ref[...] = x + _dot(merged.astype(BF16), wo_ref[...])


def _merge(x, gain, y_rwkv, y_att, y_mem, w_gate, b_gate, w_r, w_a, w_m, w_out, layer):
    t, d = x.shape
    tm = TM_MERGE
    tok = lambda width: pl.BlockSpec((tm, width), lambda i: (i, 0))
    return pl.pallas_call(
        _merge_kernel,
        grid=(t // tm,),
        in_specs=[tok(d), _const_spec((1, d)), tok(RWKV_WIDTH), tok(ATT_WIDTH), tok(MEM_WIDTH),
                  _layer_spec(w_gate.shape, layer), _const_spec((1, 3 * d)),
                  _layer_spec(w_r.shape, layer), _layer_spec(w_a.shape, layer),
                  _layer_spec(w_m.shape, layer), _layer_spec(w_out.shape, layer)],
        out_specs=tok(d),
        out_shape=jax.ShapeDtypeStruct((t, d), F32),
        compiler_params=_params(1),
        name="merge",
    )(x, gain.reshape(1, d), y_rwkv, y_att, y_mem, w_gate, b_gate.reshape(1, 3 * d), w_r, w_a, w_m,
      w_out)


def _block_diag_const(width, group, value):
    idx = jnp.arange(width) // group
    return jnp.where(idx[:, None] == idx[None, :], value, 0.0).astype(BF16)


def _chunk_tri(rows):
    i = jnp.arange(rows)
    same_chunk = (i[:, None] // CHUNK) == (i[None, :] // CHUNK)
    return (same_chunk & (i[None, :] <= i[:, None])).astype(BF16)


def kernel(x, mem, norm_ffn1, ffn1_w_in, ffn1_w_out, norm_mix, w_in, shift_mu, decay_w0, decay_lora_b, iclr_a0, iclr_lora_b, gate_lora_b, rwkv_k_k, rwkv_k_a, rwkv_r_k, rwkv_gn_g, rwkv_gn_b, vres_v0, vres_lora_a, vres_lora_b, att_q_norm, att_k_norm, att_rel_bias, norm_mem, mem_w_kv, mem_q_norm, mem_k_norm, w_branch_rwkv, w_branch_att, w_branch_mem, w_gate, b_gate, w_out, norm_ffn2, ffn2_w_in, ffn2_w_out):
    bsz, seq, d = x.shape
    assert d == D_MODEL and all(seq % tile == 0 for tile in (TM_FFN, TM_PROJ, TM_MERGE, TT_RWKV, TQ_ATT, TM_MEM))
    t = bsz * seq
    w_ = RWKV_WIDTH
    depth = w_in.shape[0]
    xf = x.reshape(t, d)

    g64 = _block_diag_const(MXU_DIM, RWKV_HEAD_DIM, 1.0 / RWKV_HEAD_DIM)
    ones64 = _block_diag_const(MXU_DIM, RWKV_HEAD_DIM, 1.0)
    g128 = _block_diag_const(MXU_DIM, MEM_HEAD_DIM, 1.0 / MEM_HEAD_DIM)
    tri = _chunk_tri(MXU_DIM)

    mk_all, mv_all = _memkv(mem.reshape(bsz * mem.shape[1], d), norm_mem, mem_w_kv.astype(BF16),
                            jnp.tile(mem_k_norm, (1, MEM_HEADS)), g128)

    ffn1_in, ffn1_out = ffn1_w_in.astype(BF16), ffn1_w_out.astype(BF16)
    ffn2_in, ffn2_out = ffn2_w_in.astype(BF16), ffn2_w_out.astype(BF16)
    w_proj, wg, wo = w_in.astype(BF16), w_gate.astype(BF16), w_out.astype(BF16)
    wbr, wba, wbm = w_branch_rwkv.astype(BF16), w_branch_att.astype(BF16), w_branch_mem.astype(BF16)

    v_first = None
    for l in range(depth):
        xf = _ffn(xf, norm_ffn1[l], ffn1_in, ffn1_out, l)

        use_vres = l > 0
        vres_a = vres_lora_a[l - 1] if use_vres else jnp.zeros((d, VRES_RANK), F32)
        vres_a = jnp.concatenate([vres_a, jnp.zeros((d, LANES - VRES_RANK), F32)], axis=1).astype(BF16)
        qk_gain = jnp.concatenate([jnp.tile(att_q_norm[l], ATT_HEADS) * (ATT_HEAD_DIM ** -0.5 * LOG2_E),
                                   jnp.tile(att_k_norm[l], ATT_HEADS)]).reshape(1, 2 * ATT_WIDTH)
        mq_gain = (jnp.tile(mem_q_norm[l], MEM_HEADS) * (MEM_HEAD_DIM ** -0.5)).reshape(1, MEM_WIDTH)
        p1, qkv, mq = _proj(xf, norm_mix[l], w_proj, vres_a, qk_gain, mq_gain, g64, g128, l)

        zeros_r = jnp.zeros((DECAY_RANK, w_), F32)
        prm = dict(
            mu=jnp.concatenate([shift_mu[l], jnp.zeros((LANES,), F32)]).reshape(1, RWKV_EXT),
            w0=decay_w0[l].reshape(1, w_), a0=iclr_a0[l].reshape(1, w_),
            lora=jnp.concatenate([jnp.concatenate([decay_lora_b[l], zeros_r], axis=1),
                                  jnp.concatenate([zeros_r, iclr_lora_b[l]], axis=1)], axis=0).astype(BF16),
            gate_b=gate_lora_b[l].astype(BF16),
            k_k=rwkv_k_k[l].reshape(1, w_), k_a=rwkv_k_a[l].reshape(1, w_),
            r_k=rwkv_r_k[l].reshape(1, w_), gn_g=rwkv_gn_g[l].reshape(1, w_),
            gn_b=rwkv_gn_b[l].reshape(1, w_), g64=g64, ones64=ones64, tri=tri)
        if use_vres:
            prm["v0"] = vres_v0[l - 1].reshape(1, w_)
            prm["vres_b"] = jnp.concatenate(
                [vres_lora_b[l - 1], jnp.zeros((LANES - VRES_RANK, w_), F32)], axis=0).astype(BF16)
        y_rwkv, v_first = _rwkv(p1.reshape(bsz, seq, RWKV_EXT), v_first, prm, use_vres)

        y_att = _att(qkv, _att_bias(att_rel_bias[l]), seq)
        y_mem = _memattn(mq, mk_all[l], mv_all[l], seq)

        xf = _merge(xf, norm_mix[l], y_rwkv.reshape(t, w_), y_att, y_mem, wg, b_gate[l], wbr, wba, wbm,
                    wo, l)
        xf = _ffn(xf, norm_ffn2[l], ffn2_in, ffn2_out, l)
    return xf.reshape(bsz, seq, d)
```

```python
import functools
import math

import jax
import jax.numpy as jnp
import numpy as np
from jax import lax
from jax.experimental import pallas as pl
from jax.experimental.pallas import tpu as pltpu

F32 = jnp.float32
BF16 = jnp.bfloat16

V7X_VMEM_LIMIT_BYTES = 56 * 1024 * 1024

D_MODEL = 1024
DEPTH = 4
CHUNK = 64
RWKV_HEADS = 8
RWKV_HEAD_DIM = 64
RWKV_WIDTH = RWKV_HEADS * RWKV_HEAD_DIM
DECAY_RANK = 64
ICLR_RANK = 64
GATE_RANK = 128
VRES_RANK = 32
ATT_HEADS = 8
ATT_HEAD_DIM = 64
ATT_WIDTH = ATT_HEADS * ATT_HEAD_DIM
LEFT_CHUNKS = 8
REL_MIN = -(CHUNK - 1)
REL_MAX = 128
N_REL = REL_MAX - REL_MIN + 1
MEM_HEADS = 4
MEM_HEAD_DIM = 128
MEM_WIDTH = MEM_HEADS * MEM_HEAD_DIM
D_FF = 2816
RMS_EPS = 1e-6
GN_EPS = 64e-5
L2_EPS = 1e-12
NEG_INF = -1e30

RWKV_IN = 3 * RWKV_WIDTH + DECAY_RANK + ICLR_RANK + GATE_RANK
LANES = 128
MXU_DIM = 256
RWKV_EXT = RWKV_IN + LANES
ATT_IN = 3 * ATT_WIDTH
HEAD_PAIRS = RWKV_WIDTH // LANES

TM_FFN = 512
TM_PROJ = 512
TM_MERGE = 512
TT_RWKV = 256
TQ_ATT = 512
TH_ATT = 128
KH_ATT = TH_ATT + LEFT_CHUNKS * CHUNK
PAIRS_PER_STAGE = 4
LOG2_E = 1.4426950408889634
TM_MEM = 2048


def _dot(a, b):
    return jnp.dot(a, b, preferred_element_type=F32)


def _dot_nt(a, b):
    return lax.dot_general(a, b, (((1,), (1,)), ((), ())), preferred_element_type=F32)


def _rms(x, gain):
    return x * lax.rsqrt(jnp.mean(x * x, axis=-1, keepdims=True) + RMS_EPS) * gain


def _group_sum(z, g):
    zb = z.astype(BF16)
    return jnp.concatenate([_dot(zb[:, i:i + MXU_DIM], g) for i in range(0, z.shape[1], MXU_DIM)],
                           axis=1)


def _const_spec(shape):
    nd = len(shape)
    return pl.BlockSpec(shape, lambda *_: (0,) * nd, pipeline_mode=pl.Buffered(1))


def _layer_spec(stacked_shape, layer):
    nd = len(stacked_shape) - 1
    return pl.BlockSpec((None,) + tuple(stacked_shape[1:]), lambda *_: (layer,) + (0,) * nd,
                        pipeline_mode=pl.Buffered(1))


def _params(n_axes):
    return pltpu.CompilerParams(dimension_semantics=("arbitrary",) * n_axes,
                                vmem_limit_bytes=V7X_VMEM_LIMIT_BYTES)


def _ffn_kernel(x_ref, g_ref, win_ref, wout_ref, o_ref):
    x = x_ref[...]
    h = _rms(x, g_ref[...]).astype(BF16)
    gu = _dot(h, win_ref[...])
    gate = gu[:, :D_FF]
    up = gu[:, D_FF:]
    act = (gate * jax.nn.sigmoid(gate) * up).astype(BF16)
    o_ref[...] = x + 0.5 * _dot(act, wout_ref[...])


def _ffn(x, gain, w_in, w_out, layer):
    t, d = x.shape
    return pl.pallas_call(
        _ffn_kernel,
        grid=(t // TM_FFN,),
        in_specs=[pl.BlockSpec((TM_FFN, d), lambda i: (i, 0)),
                  _const_spec((1, d)),
                  _layer_spec(w_in.shape, layer),
                  _layer_spec(w_out.shape, layer)],
        out_specs=pl.BlockSpec((TM_FFN, d), lambda i: (i, 0)),
        out_shape=jax.ShapeDtypeStruct((t, d), F32),
        compiler_params=_params(1),
        name="ffn",
    )(x, gain.reshape(1, d), w_in, w_out)


def _proj_kernel(x_ref, g_ref, w_ref, vres_ref, qkg_ref, mqg_ref, g64_ref, g128_ref,
                 p1_ref, qkv_ref, mq_ref):
    h = _rms(x_ref[...], g_ref[...]).astype(BF16)
    p1_ref[:, :RWKV_IN] = _dot(h, w_ref[:, :RWKV_IN])
    p1_ref[:, RWKV_IN:] = _dot(h, vres_ref[...])
    qkv = _dot(h, w_ref[:, RWKV_IN:RWKV_IN + ATT_IN])
    qk = qkv[:, :2 * ATT_WIDTH]
    ms = _group_sum(qk * qk, g64_ref[...])
    qkv_ref[:, :2 * ATT_WIDTH] = (qk * lax.rsqrt(ms + RMS_EPS) * qkg_ref[...]).astype(BF16)
    qkv_ref[:, 2 * ATT_WIDTH:] = qkv[:, 2 * ATT_WIDTH:].astype(BF16)
    mq = _dot(h, w_ref[:, RWKV_IN + ATT_IN:])
    ms = _group_sum(mq * mq, g128_ref[...])
    mq_ref[...] = (mq * lax.rsqrt(ms + RMS_EPS) * mqg_ref[...]).astype(BF16)


def _proj(x, gain, w, vres_a, qk_gain, mq_gain, g64, g128, layer):
    t, d = x.shape
    tm = TM_PROJ
    return pl.pallas_call(
        _proj_kernel,
        grid=(t // tm,),
        in_specs=[pl.BlockSpec((tm, d), lambda i: (i, 0)),
                  _const_spec((1, d)),
                  _layer_spec(w.shape, layer), _const_spec(vres_a.shape),
                  _const_spec(qk_gain.shape), _const_spec(mq_gain.shape),
                  _const_spec(g64.shape), _const_spec(g128.shape)],
        out_specs=[pl.BlockSpec((tm, RWKV_EXT), lambda i: (i, 0)),
                   pl.BlockSpec((tm, ATT_IN), lambda i: (i, 0)),
                   pl.BlockSpec((tm, MEM_WIDTH), lambda i: (i, 0))],
        out_shape=[jax.ShapeDtypeStruct((t, RWKV_EXT), F32),
                   jax.ShapeDtypeStruct((t, ATT_IN), BF16),
                   jax.ShapeDtypeStruct((t, MEM_WIDTH), BF16)],
        compiler_params=_params(1),
        name="proj",
    )(x, gain.reshape(1, d), w, vres_a, qk_gain, mq_gain, g64, g128)


def _pair_blockdiag(x, lane_lo):
    zero = jnp.zeros((), x.dtype)
    return jnp.concatenate([jnp.where(lane_lo, x, zero), jnp.where(lane_lo, zero, x)], axis=0)


def _rwkv_kernel(*refs, use_vres, n_batch):
    if use_vres:
        (p_ref, vf_ref, mu_ref, w0_ref, a0_ref, lora_ref, gateb_ref, kk_ref, ka_ref, rk_ref,
         gng_ref, gnb_ref, v0_ref, vresb_ref, g64_ref, ones64_ref, tri_ref,
         y_ref,
         carry_ref, state_ref, rt_ref, at_ref, bt_ref, kt_ref, v_ref, bkt_ref, wcol_ref,
         tat_ref, arb_ref, ul_ref, yl_ref, ys_ref) = refs
        vfo_ref = None
    else:
        (p_ref, mu_ref, w0_ref, a0_ref, lora_ref, gateb_ref, kk_ref, ka_ref, rk_ref,
         gng_ref, gnb_ref, g64_ref, ones64_ref, tri_ref,
         y_ref, vfo_ref,
         carry_ref, state_ref, rt_ref, at_ref, bt_ref, kt_ref, v_ref, bkt_ref, wcol_ref,
         tat_ref, arb_ref, ul_ref, yl_ref, ys_ref) = refs
    tt = p_ref.shape[1]
    rows_all = n_batch * tt
    n_chunks = tt // CHUNK
    n_units = n_batch * n_chunks
    w_ = RWKV_WIDTH

    @pl.when(pl.program_id(0) == 0)
    def _():
        carry_ref[...] = jnp.zeros_like(carry_ref)
        state_ref[...] = jnp.zeros_like(state_ref)

    p = p_ref[...].reshape(rows_all, RWKV_EXT)
    row = lax.broadcasted_iota(jnp.int32, p.shape, 0)
    prev = pltpu.roll(p, 1, axis=0)
    for b in range(n_batch):
        prev = jnp.where(row == b * tt, carry_ref[b:b + 1, :], prev)
    for b in range(n_batch):
        carry_ref[b:b + 1, :] = p[(b + 1) * tt - 1:(b + 1) * tt, :]
    xs = p + (prev - p) * mu_ref[...]

    r = xs[:, 0:w_]
    k = xs[:, w_:2 * w_]
    v = xs[:, 2 * w_:3 * w_]
    lw = xs[:, 3 * w_:3 * w_ + LANES]
    xg = xs[:, 3 * w_ + LANES:3 * w_ + 2 * LANES]

    lane = lax.broadcasted_iota(jnp.int32, lw.shape, 1)
    z = jnp.where(lane < DECAY_RANK, jnp.tanh(lw), lw).astype(BF16)
    lo = _dot(z, lora_ref[...])
    logw = F32(-LOG2_E * math.exp(-0.5)) * jax.nn.sigmoid(w0_ref[...] + lo[:, :w_])
    a = jax.nn.sigmoid(a0_ref[...] + lo[:, w_:])
    g = _dot(jax.nn.sigmoid(xg).astype(BF16), gateb_ref[...])

    if use_vres:
        hv = xs[:, RWKV_IN:RWKV_EXT].astype(BF16)
        mix = jax.nn.sigmoid(v0_ref[...] + _dot(hv, vresb_ref[...]))
        v = v + (vf_ref[...].reshape(rows_all, w_) - v) * mix
    else:
        vfo_ref[...] = v.reshape(n_batch, tt, w_)

    kk = k * kk_ref[...]
    ss = _group_sum(kk * kk, ones64_ref[...])
    kk = kk * lax.rsqrt(jnp.maximum(ss, L2_EPS * L2_EPS))
    k2 = k * (1.0 + (a - 1.0) * ka_ref[...])
    bvec = kk * a

    tri = tri_ref[...]
    hi = logw.astype(BF16)
    mid = (logw - hi.astype(F32)).astype(BF16)
    cum = jnp.concatenate([_dot(tri, hi[i:i + MXU_DIM]) + _dot(tri, mid[i:i + MXU_DIM])
                           for i in range(0, rows_all, MXU_DIM)], axis=0)
    wcum = jnp.exp2(cum)
    inv_w = jnp.exp2(-cum)
    bt = bvec * inv_w
    kt = k2 * inv_w
    rt_ref[...] = (r * wcum).astype(BF16)
    at_ref[...] = (-kk * jnp.exp2(cum - logw)).astype(BF16)
    bt_ref[...] = bt.astype(BF16)
    kt_ref[...] = kt.astype(BF16)
    v_ref[...] = v.astype(BF16)
    for u in range(n_units):
        lo_row, hi_row = u * CHUNK, (u + 1) * CHUNK
        bkt_ref[u] = jnp.concatenate([bt[lo_row:hi_row], kt[lo_row:hi_row]], axis=0).T.astype(BF16)
        wcol_ref[u] = jnp.broadcast_to(wcum[hi_row - 1:hi_row, :], (LANES, w_)).T

    li = lax.broadcasted_iota(jnp.int32, (CHUNK, LANES), 0)
    mi = lax.broadcasted_iota(jnp.int32, (CHUNK, LANES), 1)
    lane_lo = mi < RWKV_HEAD_DIM
    mloc = jnp.where(lane_lo, mi, mi - RWKV_HEAD_DIM)
    strict = mloc < li
    incl = mloc <= li
    eye2 = (mloc == li).astype(F32)
    bi = lax.broadcasted_iota(jnp.int32, (LANES, LANES), 0)
    bj = lax.broadcasted_iota(jnp.int32, (LANES, LANES), 1)
    diag_blocks = (bi < RWKV_HEAD_DIM) == (bj < RWKV_HEAD_DIM)

    def tile_of(unit, pair):
        return (slice(unit * CHUNK, (unit + 1) * CHUNK), slice(pair * LANES, (pair + 1) * LANES))

    tiles = [tile_of(u, pr) for u in range(n_units) for pr in range(HEAD_PAIRS)]
    pw, tinv, grams, rhsl = [], [], [], []
    for rows, cols in tiles:
        at2, rt2, bt2, kt2 = (ref[rows, cols] for ref in (at_ref, rt_ref, bt_ref, kt_ref))
        grams.append(_dot_nt(jnp.concatenate([at2, rt2], axis=0),
                             jnp.concatenate([_pair_blockdiag(bt2, lane_lo), _pair_blockdiag(kt2, lane_lo)],
                                             axis=0)))
    for (rows, cols), gram in zip(tiles, grams):
        v2 = v_ref[rows, cols]
        a_ab = jnp.where(strict, gram[:CHUNK, :LANES], 0.0)
        a_ak = jnp.where(strict, gram[:CHUNK, LANES:], 0.0)
        a_rb = jnp.where(incl, gram[CHUNK:, :LANES], 0.0)
        a_rk = jnp.where(incl, gram[CHUNK:, LANES:], 0.0)
        loc = _dot(jnp.concatenate([a_ak, a_rk], axis=0).astype(BF16), _pair_blockdiag(v2, lane_lo))
        rhsl.append(loc[:CHUNK].astype(BF16))
        yl_ref[rows, cols] = loc[CHUNK:]
        arb_ref[rows, cols] = a_rb.astype(BF16)
        pw.append(a_ab)
        tinv.append(eye2 + a_ab)
    pwb = [x.astype(BF16) for x in pw]
    pw = [_dot(x, _pair_blockdiag(x, lane_lo)) for x in pwb]
    for level in range(5):
        pwb = [x.astype(BF16) for x in pw]
        if level < 4:
            both = [_dot(jnp.concatenate([x, tv.astype(BF16)], axis=0), _pair_blockdiag(x, lane_lo))
                    for x, tv in zip(pwb, tinv)]
            pw = [m[:CHUNK] for m in both]
            tinv = [tv + m[CHUNK:] for tv, m in zip(tinv, both)]
        else:
            tinv = [tv + _dot(tv.astype(BF16), _pair_blockdiag(x, lane_lo))
                    for x, tv in zip(pwb, tinv)]
    for (rows, cols), tv, x in zip(tiles, tinv, rhsl):
        both = _dot(tv.astype(BF16), jnp.concatenate([_pair_blockdiag(at_ref[rows, cols], lane_lo),
                                                      _pair_blockdiag(x, lane_lo)], axis=1))
        tat_ref[rows, cols] = both[:, :LANES].astype(BF16)
        ul_ref[rows, cols] = both[:, LANES:]

    chains = [(b, pr) for b in range(n_batch) for pr in range(HEAD_PAIRS)]
    st = [state_ref[b * HEAD_PAIRS + pr] for b, pr in chains]
    for c in range(n_chunks):
        units = [b * n_chunks + c for b, _ in chains]
        tiles = [tile_of(u, pr) for u, (_, pr) in zip(units, chains)]
        m1 = [_dot(jnp.concatenate([tat_ref[rows, cols], rt_ref[rows, cols]], axis=0), s.astype(BF16))
              for (rows, cols), s in zip(tiles, st)]
        ub = [(m[:CHUNK] + ul_ref[rows, cols]).astype(BF16) for m, (rows, cols) in zip(m1, tiles)]
        new_st = []
        for u, (_, pr), (rows, cols), s, x in zip(units, chains, tiles, st, ub):
            chans = slice(pr * LANES, (pr + 1) * LANES)
            upd = _dot(bkt_ref[u, chans, :], jnp.concatenate([x, v_ref[rows, cols]], axis=0))
            new_st.append((s + jnp.where(diag_blocks, upd, 0.0)) * wcol_ref[u, chans, :])
        st = new_st
        for m, (rows, cols), x in zip(m1, tiles, ub):
            ys_ref[rows, cols] = (m[CHUNK:] + _dot(arb_ref[rows, cols], _pair_blockdiag(x, lane_lo))
                                  + yl_ref[rows, cols])
    for (b, pr), s in zip(chains, st):
        state_ref[b * HEAD_PAIRS + pr] = s

    y = ys_ref[...]
    mean = _group_sum(y, g64_ref[...])
    yc = y - mean
    var = _group_sum(yc * yc, g64_ref[...])
    yn = yc * lax.rsqrt(var + GN_EPS) * gng_ref[...] + gnb_ref[...]
    bonus = _group_sum(r * k2 * rk_ref[...], ones64_ref[...]) * v
    y_ref[...] = ((yn + bonus) * g).astype(BF16).reshape(n_batch, tt, w_)


def _rwkv(p1, v_first, prm, use_vres):
    n_batch, seq, _ = p1.shape
    tt = TT_RWKV
    rows_all = n_batch * tt
    n_units = rows_all // CHUNK
    w_ = RWKV_WIDTH
    tok = lambda width: pl.BlockSpec((n_batch, tt, width), lambda i: (0, i, 0))
    vec = _const_spec((1, w_))
    in_specs = [tok(RWKV_EXT)]
    args = [p1]
    if use_vres:
        in_specs.append(tok(w_))
        args.append(v_first)
    in_specs += [_const_spec((1, RWKV_EXT)), vec, vec, _const_spec((LANES, 2 * w_)),
                 _const_spec((GATE_RANK, w_)), vec, vec, vec, vec, vec]
    args += [prm["mu"], prm["w0"], prm["a0"], prm["lora"], prm["gate_b"], prm["k_k"], prm["k_a"],
             prm["r_k"], prm["gn_g"], prm["gn_b"]]
    if use_vres:
        in_specs += [vec, _const_spec((LANES, w_))]
        args += [prm["v0"], prm["vres_b"]]
    in_specs += [_const_spec((MXU_DIM, MXU_DIM))] * 3
    args += [prm["g64"], prm["ones64"], prm["tri"]]
    out_specs = [tok(w_)]
    out_shape = [jax.ShapeDtypeStruct((n_batch, seq, w_), BF16)]
    if not use_vres:
        out_specs.append(tok(w_))
        out_shape.append(jax.ShapeDtypeStruct((n_batch, seq, w_), F32))
    act_bf16 = pltpu.VMEM((rows_all, w_), BF16)
    act_f32 = pltpu.VMEM((rows_all, w_), F32)
    scratch = [pltpu.VMEM((8, RWKV_EXT), F32),
               pltpu.VMEM((n_batch * HEAD_PAIRS, LANES, LANES), F32),
               act_bf16, act_bf16, act_bf16, act_bf16, act_bf16,
               pltpu.VMEM((n_units, w_, LANES), BF16),
               pltpu.VMEM((n_units, w_, LANES), F32),
               act_bf16, act_bf16, act_f32, act_f32, act_f32]
    out = pl.pallas_call(
        functools.partial(_rwkv_kernel, use_vres=use_vres, n_batch=n_batch),
        grid=(seq // tt,),
        in_specs=in_specs,
        out_specs=out_specs,
        out_shape=out_shape,
        scratch_shapes=scratch,
        compiler_params=_params(1),
        name="rwkv",
    )(*args)
    if use_vres:
        return out[0], v_first
    return out[0], out[1]


def _att_kernel(q_ref, k0_ref, k1_ref, v0_ref, v1_ref, *rest):
    n_half = TQ_ATT // TH_ATT
    bias_refs, o_ref = rest[:n_half], rest[n_half]
    lane = lax.broadcasted_iota(jnp.int32, (1, LANES), 1)
    zero = jnp.zeros((), BF16)
    head_lanes = [(lane >= sub * ATT_HEAD_DIM) & (lane < (sub + 1) * ATT_HEAD_DIM) for sub in range(2)]
    den_lane = [(1 - sub) * ATT_HEAD_DIM for sub in range(2)]
    ones_col = [jnp.where(lane == den_lane[sub], 1.0, 0.0).astype(BF16) for sub in range(2)]
    for group in range(ATT_HEADS // 2 // PAIRS_PER_STAGE):
        units = [(hp, half, sub)
                 for hp in range(group * PAIRS_PER_STAGE, (group + 1) * PAIRS_PER_STAGE)
                 for half in range(n_half) for sub in range(2)]
        k_win, v_win = {}, {}
        for hp in range(group * PAIRS_PER_STAGE, (group + 1) * PAIRS_PER_STAGE):
            cols = slice(hp * LANES, (hp + 1) * LANES)
            k_win[hp] = jnp.concatenate([k0_ref[:, cols], k1_ref[:, cols]], axis=0)
            v_win[hp] = jnp.concatenate([v0_ref[:, cols], v1_ref[:, cols]], axis=0)
        scores = []
        for hp, half, sub in units:
            q2 = q_ref[half * TH_ATT:(half + 1) * TH_ATT, hp * LANES:(hp + 1) * LANES]
            k2 = k_win[hp][half * TH_ATT:half * TH_ATT + KH_ATT]
            scores.append(_dot_nt(jnp.where(head_lanes[sub], q2, zero), k2)
                          + bias_refs[half][0, 2 * hp + sub])
        probs = [jnp.exp2(s - jnp.max(s, axis=-1, keepdims=True)).astype(BF16) for s in scores]
        pvs = []
        for (hp, half, sub), e in zip(units, probs):
            v2 = v_win[hp][half * TH_ATT:half * TH_ATT + KH_ATT]
            pvs.append(_dot(e, jnp.where(head_lanes[sub], v2, zero) + ones_col[sub]))
        for j in range(0, len(units), 2):
            hp, half, _ = units[j]
            lo, hi = pvs[j], pvs[j + 1]
            out = jnp.where(head_lanes[0], lo / lo[:, den_lane[0]:den_lane[0] + 1],
                            hi / hi[:, den_lane[1]:den_lane[1] + 1])
            o_ref[half * TH_ATT:(half + 1) * TH_ATT, hp * LANES:(hp + 1) * LANES] = out.astype(BF16)


def _att(qkv, bias, seq):
    t = qkv.shape[0]
    tq = TQ_ATT
    nq = seq // tq
    blk = lambda col, back: pl.BlockSpec(
        (tq, ATT_WIDTH), lambda b, i: (b * nq + jnp.maximum(i - back, 0), col))
    n_half = tq // TH_ATT
    bias_specs = [pl.BlockSpec((1,) + bias.shape[1:],
                               lambda b, i, half=half: (jnp.where(i == 0, half, n_half), 0, 0, 0))
                  for half in range(n_half)]
    return pl.pallas_call(
        _att_kernel,
        grid=(t // seq, nq),
        in_specs=[blk(0, 0), blk(1, 1), blk(1, 0), blk(2, 1), blk(2, 0)] + bias_specs,
        out_specs=pl.BlockSpec((tq, ATT_WIDTH), lambda b, i: (b * nq + i, 0)),
        out_shape=jax.ShapeDtypeStruct((t, ATT_WIDTH), BF16),
        compiler_params=_params(2),
        name="band_att",
    )(qkv, qkv, qkv, qkv, qkv, *([bias] * n_half))


def _att_bias(rel_table):
    heads = rel_table.shape[0]
    far = rel_table[:, N_REL - 1:]
    near = rel_table[:, :1]
    n_far = KH_ATT - TH_ATT - REL_MAX
    period = KH_ATT + TH_ATT + 1
    base = jnp.concatenate([
        jnp.broadcast_to(far, (heads, n_far)),
        rel_table[:, ::-1],
        jnp.broadcast_to(near, (heads, KH_ATT - n_far - N_REL)),
        jnp.broadcast_to(far, (heads, period - KH_ATT)),
    ], axis=1)
    rows = jnp.tile(base, (1, TH_ATT))[:, :TH_ATT * (period - 1)].reshape(heads, TH_ATT, period - 1)
    bias = rows[:, :, :KH_ATT].astype(F32) * LOG2_E
    qc = np.arange(TH_ATT)[:, None] // CHUNK
    kcol = np.arange(KH_ATT)[None, :]
    in_band = (kcol // CHUNK >= qc) & (kcol // CHUNK <= qc + LEFT_CHUNKS)
    n_half = TQ_ATT // TH_ATT
    first_valid = [max(KH_ATT - TH_ATT - half * TH_ATT, 0) for half in range(n_half)] + [0]
    keep = np.stack([in_band & (kcol >= fv) for fv in first_valid])
    return jnp.where(keep[:, None], bias[None], NEG_INF)


def _memkv_kernel(mem_ref, g_ref, w_ref, kg_ref, g128_ref, mk_ref, mv_ref):
    h = _rms(mem_ref[...], g_ref[0]).astype(BF16)
    kv = _dot(h, w_ref[0])
    mk = kv[:, :MEM_WIDTH]
    ms = _group_sum(mk * mk, g128_ref[...])
    mk_ref[0] = (mk * lax.rsqrt(ms + RMS_EPS) * kg_ref[0]).astype(BF16)
    mv_ref[0] = kv[:, MEM_WIDTH:].astype(BF16)


def _memkv(mem, norm_mem, w_kv, k_gain, g128):
    m, d = mem.shape
    depth = w_kv.shape[0]
    return pl.pallas_call(
        _memkv_kernel,
        grid=(depth,),
        in_specs=[_const_spec((m, d)),
                  pl.BlockSpec((1, 1, d), lambda l: (l, 0, 0)),
                  pl.BlockSpec((1, d, 2 * MEM_WIDTH), lambda l: (l, 0, 0)),
                  pl.BlockSpec((1, 1, MEM_WIDTH), lambda l: (l, 0, 0)),
                  _const_spec(g128.shape)],
        out_specs=[pl.BlockSpec((1, m, MEM_WIDTH), lambda l: (l, 0, 0)),
                   pl.BlockSpec((1, m, MEM_WIDTH), lambda l: (l, 0, 0))],
        out_shape=[jax.ShapeDtypeStruct((depth, m, MEM_WIDTH), BF16),
                   jax.ShapeDtypeStruct((depth, m, MEM_WIDTH), BF16)],
        compiler_params=_params(1),
        name="mem_kv",
    )(mem, norm_mem.reshape(depth, 1, d), w_kv, k_gain.reshape(depth, 1, MEM_WIDTH), g128)


def _memattn_kernel(q_ref, k_ref, v_ref, o_ref):
    heads = [slice(h * MEM_HEAD_DIM, (h + 1) * MEM_HEAD_DIM) for h in range(MEM_HEADS)]
    scores = [_dot_nt(q_ref[:, cols], k_ref[:, cols]) for cols in heads]
    probs = [jnp.exp(s - jnp.max(s, axis=-1, keepdims=True)) for s in scores]
    denoms = [jnp.sum(e, axis=-1, keepdims=True) for e in probs]
    pvs = [_dot(e.astype(BF16), v_ref[:, cols]) for e, cols in zip(probs, heads)]
    for cols, pv, denom in zip(heads, pvs, denoms):
        o_ref[:, cols] = (pv / denom).astype(BF16)


def _memattn(mq, mk, mv, seq):
    t = mq.shape[0]
    tm = TM_MEM
    ns = seq // tm
    m = mk.shape[0] // (t // seq)
    kv_spec = pl.BlockSpec((m, MEM_WIDTH), lambda b, i: (b, 0))
    return pl.pallas_call(
        _memattn_kernel,
        grid=(t // seq, ns),
        in_specs=[pl.BlockSpec((tm, MEM_WIDTH), lambda b, i: (b * ns + i, 0)), kv_spec, kv_spec],
        out_specs=pl.BlockSpec((tm, MEM_WIDTH), lambda b, i: (b * ns + i, 0)),
        out_shape=jax.ShapeDtypeStruct((t, MEM_WIDTH), BF16),
        compiler_params=_params(2),
        name="mem_att",
    )(mq, mk, mv)


def _merge_kernel(x_ref, ng_ref, yr_ref, ya_ref, ym_ref, wg_ref, bg_ref, wr_ref, wa_ref, wm_ref,
                  wo_ref, o_ref):
    x = x_ref[...]
    d = x.shape[1]
    h = _rms(x, ng_ref[...]).astype(BF16)
    gates = jax.nn.sigmoid(_dot(h, wg_ref[...]) + bg_ref[...])
    merged = (gates[:, 0:d] * _dot(yr_ref[...], wr_ref[...])
              + gates[:, d:2 * d] * _dot(ya_ref[...], wa_ref[...])
              + gates[:, 2 * d:3 * d] * _dot(ym_ref[...], wm_ref[...]))
    o_ref[...] = x + _dot(merged.astype(BF16), wo_ref[...])


def _merge(x, gain, y_rwkv, y_att, y_mem, w_gate, b_gate, w_r, w_a, w_m, w_out, layer):
    t, d = x.shape
    tm = TM_MERGE
    tok = lambda width: pl.BlockSpec((tm, width), lambda i: (i, 0))
    return pl.pallas_call(
        _merge_kernel,
        grid=(t // tm,),
        in_specs=[tok(d), _const_spec((1, d)), tok(RWKV_WIDTH), tok(ATT_WIDTH), tok(MEM_WIDTH),
                  _layer_spec(w_gate.shape, layer), _const_spec((1, 3 * d)),
                  _layer_spec(w_r.shape, layer), _layer_spec(w_a.shape, layer),
                  _layer_spec(w_m.shape, layer), _layer_spec(w_out.shape, layer)],
        out_specs=tok(d),
        out_shape=jax.ShapeDtypeStruct((t, d), F32),
        compiler_params=_params(1),
        name="merge",
    )(x, gain.reshape(1, d), y_rwkv, y_att, y_mem, w_gate, b_gate.reshape(1, 3 * d), w_r, w_a, w_m,
      w_out)


def _block_diag_const(width, group, value):
    idx = jnp.arange(width) // group
    return jnp.where(idx[:, None] == idx[None, :], value, 0.0).astype(BF16)


def _chunk_tri(rows):
    i = jnp.arange(rows)
    same_chunk = (i[:, None] // CHUNK) == (i[None, :] // CHUNK)
    return (same_chunk & (i[None, :] <= i[:, None])).astype(BF16)


def kernel(x, mem, norm_ffn1, ffn1_w_in, ffn1_w_out, norm_mix, w_in, shift_mu, decay_w0, decay_lora_b, iclr_a0, iclr_lora_b, gate_lora_b, rwkv_k_k, rwkv_k_a, rwkv_r_k, rwkv_gn_g, rwkv_gn_b, vres_v0, vres_lora_a, vres_lora_b, att_q_norm, att_k_norm, att_rel_bias, norm_mem, mem_w_kv, mem_q_norm, mem_k_norm, w_branch_rwkv, w_branch_att, w_branch_mem, w_gate, b_gate, w_out, norm_ffn2, ffn2_w_in, ffn2_w_out):
    bsz, seq, d = x.shape
    assert d == D_MODEL and all(seq % tile == 0 for tile in (TM_FFN, TM_PROJ, TM_MERGE, TT_RWKV, TQ_ATT, TM_MEM))
    t = bsz * seq
    w_ = RWKV_WIDTH
    depth = w_in.shape[0]
    xf = x.reshape(t, d)

    g64 = _block_diag_const(MXU_DIM, RWKV_HEAD_DIM, 1.0 / RWKV_HEAD_DIM)
    ones64 = _block_diag_const(MXU_DIM, RWKV_HEAD_DIM, 1.0)
    g128 = _block_diag_const(MXU_DIM, MEM_HEAD_DIM, 1.0 / MEM_HEAD_DIM)
    tri = _chunk_tri(MXU_DIM)

    mk_all, mv_all = _memkv(mem.reshape(bsz * mem.shape[1], d), norm_mem, mem_w_kv.astype(BF16),
                            jnp.tile(mem_k_norm, (1, MEM_HEADS)), g128)

    ffn1_in, ffn1_out = ffn1_w_in.astype(BF16), ffn1_w_out.astype(BF16)
    ffn2_in, ffn2_out = ffn2_w_in.astype(BF16), ffn2_w_out.astype(BF16)
    w_proj, wg, wo = w_in.astype(BF16), w_gate.astype(BF16), w_out.astype(BF16)
    wbr, wba, wbm = w_branch_rwkv.astype(BF16), w_branch_att.astype(BF16), w_branch_mem.astype(BF16)

    v_first = None
    for l in range(depth):
        xf = _ffn(xf, norm_ffn1[l], ffn1_in, ffn1_out, l)

        use_vres = l > 0
        vres_a = vres_lora_a[l - 1] if use_vres else jnp.zeros((d, VRES_RANK), F32)
        vres_a = jnp.concatenate([vres_a, jnp.zeros((d, LANES - VRES_RANK), F32)], axis=1).astype(BF16)
        qk_gain = jnp.concatenate([jnp.tile(att_q_norm[l], ATT_HEADS) * (ATT_HEAD_DIM ** -0.5 * LOG2_E),
                                   jnp.tile(att_k_norm[l], ATT_HEADS)]).reshape(1, 2 * ATT_WIDTH)
        mq_gain = (jnp.tile(mem_q_norm[l], MEM_HEADS) * (MEM_HEAD_DIM ** -0.5)).reshape(1, MEM_WIDTH)
        p1, qkv, mq = _proj(xf, norm_mix[l], w_proj, vres_a, qk_gain, mq_gain, g64, g128, l)

        zeros_r = jnp.zeros((DECAY_RANK, w_), F32)
        prm = dict(
            mu=jnp.concatenate([shift_mu[l], jnp.zeros((LANES,), F32)]).reshape(1, RWKV_EXT),
            w0=decay_w0[l].reshape(1, w_), a0=iclr_a0[l].reshape(1, w_),
            lora=jnp.concatenate([jnp.concatenate([decay_lora_b[l], zeros_r], axis=1),
                                  jnp.concatenate([zeros_r, iclr_lora_b[l]], axis=1)], axis=0).astype(BF16),
            gate_b=gate_lora_b[l].astype(BF16),
            k_k=rwkv_k_k[l].reshape(1, w_), k_a=rwkv_k_a[l].reshape(1, w_),
            r_k=rwkv_r_k[l].reshape(1, w_), gn_g=rwkv_gn_g[l].reshape(1, w_),
            gn_b=rwkv_gn_b[l].reshape(1, w_), g64=g64, ones64=ones64, tri=tri)
        if use_vres:
            prm["v0"] = vres_v0[l - 1].reshape(1, w_)
            prm["vres_b"] = jnp.concatenate(
                [vres_lora_b[l - 1], jnp.zeros((LANES - VRES_RANK, w_), F32)], axis=0).astype(BF16)
        y_rwkv, v_first = _rwkv(p1.reshape(bsz, seq, RWKV_EXT), v_first, prm, use_vres)

        y_att = _att(qkv, _att_bias(att_rel_bias[l]), seq)
        y_mem = _memattn(mq, mk_all[l], mv_all[l], seq)

        xf = _merge(xf, norm_mix[l], y_rwkv.reshape(t, w_), y_att, y_mem, wg, b_gate[l], wbr, wba, wbm,
                    wo, l)
        xf = _ffn(xf, norm_ffn2[l], ffn2_in, ffn2_out, l)
    return xf.reshape(bsz, seq, d)
```

```python
import functools
import math

import jax
import jax.numpy as jnp
import numpy as np
from jax import lax
from jax.experimental import pallas as pl
from jax.experimental.pallas import tpu as pltpu

F32 = jnp.float32
BF16 = jnp.bfloat16

V7X_VMEM_LIMIT_BYTES = 56 * 1024 * 1024

D_MODEL = 1024
DEPTH = 4
CHUNK = 64
RWKV_HEADS = 8
RWKV_HEAD_DIM = 64
RWKV_WIDTH = RWKV_HEADS * RWKV_HEAD_DIM
DECAY_RANK = 64
ICLR_RANK = 64
GATE_RANK = 128
VRES_RANK = 32
ATT_HEADS = 8
ATT_HEAD_DIM = 64
ATT_WIDTH = ATT_HEADS * ATT_HEAD_DIM
LEFT_CHUNKS = 8
REL_MIN = -(CHUNK - 1)
REL_MAX = 128
N_REL = REL_MAX - REL_MIN + 1
MEM_HEADS = 4
MEM_HEAD_DIM = 128
MEM_WIDTH = MEM_HEADS * MEM_HEAD_DIM
D_FF = 2816
RMS_EPS = 1e-6
GN_EPS = 64e-5
L2_EPS = 1e-12
NEG_INF = -1e30

RWKV_IN = 3 * RWKV_WIDTH + DECAY_RANK + ICLR_RANK + GATE_RANK
LANES = 128
MXU_DIM = 256
RWKV_EXT = RWKV_IN + LANES
ATT_IN = 3 * ATT_WIDTH
HEAD_PAIRS = RWKV_WIDTH // LANES

TM_FFN = 512
TM_PROJ = 512
TM_MERGE = 512
TT_RWKV = 256
TQ_ATT = 512
TH_ATT = 128
KH_ATT = TH_ATT + LEFT_CHUNKS * CHUNK
PAIRS_PER_STAGE = 4
LOG2_E = 1.4426950408889634


def _dot(a, b):
    return jnp.dot(a, b, preferred_element_type=F32)


def _dot_nt(a, b):
    return lax.dot_general(a, b, (((1,), (1,)), ((), ())), preferred_element_type=F32)


def _rms(x, gain):
    return x * lax.rsqrt(jnp.mean(x * x, axis=-1, keepdims=True) + RMS_EPS) * gain


def _group_sum(z, g):
    zb = z.astype(BF16)
    return jnp.concatenate([_dot(zb[:, i:i + MXU_DIM], g) for i in range(0, z.shape[1], MXU_DIM)],
                           axis=1)


def _const_spec(shape):
    nd = len(shape)
    return pl.BlockSpec(shape, lambda *_: (0,) * nd, pipeline_mode=pl.Buffered(1))


def _layer_spec(stacked_shape, layer):
    nd = len(stacked_shape) - 1
    return pl.BlockSpec((None,) + tuple(stacked_shape[1:]), lambda *_: (layer,) + (0,) * nd,
                        pipeline_mode=pl.Buffered(1))


def _params(n_axes):
    return pltpu.CompilerParams(dimension_semantics=("arbitrary",) * n_axes,
                                vmem_limit_bytes=V7X_VMEM_LIMIT_BYTES)


def _ffn_kernel(x_ref, g_ref, win_ref, wout_ref, o_ref):
    x = x_ref[...]
    h = _rms(x, g_ref[...]).astype(BF16)
    gu = _dot(h, win_ref[...])
    gate = gu[:, :D_FF]
    up = gu[:, D_FF:]
    act = (gate * jax.nn.sigmoid(gate) * up).astype(BF16)
    o_ref[...] = x + 0.5 * _dot(act, wout_ref[...])


def _ffn(x, gain, w_in, w_out, layer):
    t, d = x.shape
    return pl.pallas_call(
        _ffn_kernel,
        grid=(t // TM_FFN,),
        in_specs=[pl.BlockSpec((TM_FFN, d), lambda i: (i, 0)),
                  _const_spec((1, d)),
                  _layer_spec(w_in.shape, layer),
                  _layer_spec(w_out.shape, layer)],
        out_specs=pl.BlockSpec((TM_FFN, d), lambda i: (i, 0)),
        out_shape=jax.ShapeDtypeStruct((t, d), F32),
        compiler_params=_params(1),
        name="ffn",
    )(x, gain.reshape(1, d), w_in, w_out)


def _proj_kernel(x_ref, g_ref, w_ref, vres_ref, qkg_ref, mqg_ref, g64_ref, g128_ref, mk_ref, mv_ref,
                 p1_ref, qkv_ref, ymem_ref):
    h = _rms(x_ref[...], g_ref[...]).astype(BF16)
    mq = _dot(h, w_ref[:, RWKV_IN + ATT_IN:])
    ms = _group_sum(mq * mq, g128_ref[...])
    mq = (mq * lax.rsqrt(ms + RMS_EPS) * mqg_ref[...]).astype(BF16)
    heads = [slice(hd * MEM_HEAD_DIM, (hd + 1) * MEM_HEAD_DIM) for hd in range(MEM_HEADS)]
    scores = [_dot_nt(mq[:, cols], mk_ref[:, cols]) for cols in heads]
    p1_ref[:, :RWKV_IN] = _dot(h, w_ref[:, :RWKV_IN])
    probs = [jnp.exp(sc - jnp.max(sc, axis=-1, keepdims=True)) for sc in scores]
    denoms = [jnp.sum(e, axis=-1, keepdims=True) for e in probs]
    p1_ref[:, RWKV_IN:] = _dot(h, vres_ref[...])
    qkv = _dot(h, w_ref[:, RWKV_IN:RWKV_IN + ATT_IN])
    qk = qkv[:, :2 * ATT_WIDTH]
    ms = _group_sum(qk * qk, g64_ref[...])
    qkv_ref[:, :2 * ATT_WIDTH] = (qk * lax.rsqrt(ms + RMS_EPS) * qkg_ref[...]).astype(BF16)
    qkv_ref[:, 2 * ATT_WIDTH:] = qkv[:, 2 * ATT_WIDTH:].astype(BF16)
    pvs = [_dot(e.astype(BF16), mv_ref[:, cols]) for e, cols in zip(probs, heads)]
    for cols, pv, denom in zip(heads, pvs, denoms):
        ymem_ref[:, cols] = (pv / denom).astype(BF16)


def _proj(x, gain, w, vres_a, qk_gain, mq_gain, g64, g128, mk, mv, layer, seq):
    t, d = x.shape
    tm = TM_PROJ
    m = mk.shape[1] // (t // seq)
    kv_spec = pl.BlockSpec((None, m, MEM_WIDTH), lambda i: (layer, i // (seq // tm), 0))
    return pl.pallas_call(
        _proj_kernel,
        grid=(t // tm,),
        in_specs=[pl.BlockSpec((tm, d), lambda i: (i, 0)),
                  _const_spec((1, d)),
                  _layer_spec(w.shape, layer), _const_spec(vres_a.shape),
                  _const_spec(qk_gain.shape), _const_spec(mq_gain.shape),
                  _const_spec(g64.shape), _const_spec(g128.shape), kv_spec, kv_spec],
        out_specs=[pl.BlockSpec((tm, RWKV_EXT), lambda i: (i, 0)),
                   pl.BlockSpec((tm, ATT_IN), lambda i: (i, 0)),
                   pl.BlockSpec((tm, MEM_WIDTH), lambda i: (i, 0))],
        out_shape=[jax.ShapeDtypeStruct((t, RWKV_EXT), F32),
                   jax.ShapeDtypeStruct((t, ATT_IN), BF16),
                   jax.ShapeDtypeStruct((t, MEM_WIDTH), BF16)],
        compiler_params=_params(1),
        name="proj",
    )(x, gain.reshape(1, d), w, vres_a, qk_gain, mq_gain, g64, g128, mk, mv)


def _pair_blockdiag(x, lane_lo):
    zero = jnp.zeros((), x.dtype)
    return jnp.concatenate([jnp.where(lane_lo, x, zero), jnp.where(lane_lo, zero, x)], axis=0)


def _rwkv_kernel(*refs, use_vres, n_batch):
    if use_vres:
        (p_ref, vf_ref, mu_ref, w0_ref, a0_ref, lora_ref, gateb_ref, kk_ref, ka_ref, rk_ref,
         gng_ref, gnb_ref, v0_ref, vresb_ref, g64_ref, ones64_ref, tri_ref,
         y_ref,
         carry_ref, state_ref, rt_ref, at_ref, bt_ref, kt_ref, v_ref, bkt_ref, wcol_ref,
         tat_ref, arb_ref, ul_ref, yl_ref, ys_ref) = refs
        vfo_ref = None
    else:
        (p_ref, mu_ref, w0_ref, a0_ref, lora_ref, gateb_ref, kk_ref, ka_ref, rk_ref,
         gng_ref, gnb_ref, g64_ref, ones64_ref, tri_ref,
         y_ref, vfo_ref,
         carry_ref, state_ref, rt_ref, at_ref, bt_ref, kt_ref, v_ref, bkt_ref, wcol_ref,
         tat_ref, arb_ref, ul_ref, yl_ref, ys_ref) = refs
    tt = p_ref.shape[1]
    rows_all = n_batch * tt
    n_chunks = tt // CHUNK
    n_units = n_batch * n_chunks
    w_ = RWKV_WIDTH

    @pl.when(pl.program_id(0) == 0)
    def _():
        carry_ref[...] = jnp.zeros_like(carry_ref)
        state_ref[...] = jnp.zeros_like(state_ref)

    p = p_ref[...].reshape(rows_all, RWKV_EXT)
    row = lax.broadcasted_iota(jnp.int32, p.shape, 0)
    prev = pltpu.roll(p, 1, axis=0)
    for b in range(n_batch):
        prev = jnp.where(row == b * tt, carry_ref[b:b + 1, :], prev)
    for b in range(n_batch):
        carry_ref[b:b + 1, :] = p[(b + 1) * tt - 1:(b + 1) * tt, :]
    xs = p + (prev - p) * mu_ref[...]

    r = xs[:, 0:w_]
    k = xs[:, w_:2 * w_]
    v = xs[:, 2 * w_:3 * w_]
    lw = xs[:, 3 * w_:3 * w_ + LANES]
    xg = xs[:, 3 * w_ + LANES:3 * w_ + 2 * LANES]

    lane = lax.broadcasted_iota(jnp.int32, lw.shape, 1)
    z = jnp.where(lane < DECAY_RANK, jnp.tanh(lw), lw).astype(BF16)
    lo = _dot(z, lora_ref[...])
    logw = F32(-LOG2_E * math.exp(-0.5)) * jax.nn.sigmoid(w0_ref[...] + lo[:, :w_])
    a = jax.nn.sigmoid(a0_ref[...] + lo[:, w_:])
    g = _dot(jax.nn.sigmoid(xg).astype(BF16), gateb_ref[...])

    if use_vres:
        hv = xs[:, RWKV_IN:RWKV_EXT].astype(BF16)
        mix = jax.nn.sigmoid(v0_ref[...] + _dot(hv, vresb_ref[...]))
        v = v + (vf_ref[...].reshape(rows_all, w_) - v) * mix
    else:
        vfo_ref[...] = v.reshape(n_batch, tt, w_)

    kk = k * kk_ref[...]
    ss = _group_sum(kk * kk, ones64_ref[...])
    kk = kk * lax.rsqrt(jnp.maximum(ss, L2_EPS * L2_EPS))
    k2 = k * (1.0 + (a - 1.0) * ka_ref[...])
    bvec = kk * a

    tri = tri_ref[...]
    hi = logw.astype(BF16)
    mid = (logw - hi.astype(F32)).astype(BF16)
    cum = jnp.concatenate([_dot(tri, hi[i:i + MXU_DIM]) + _dot(tri, mid[i:i + MXU_DIM])
                           for i in range(0, rows_all, MXU_DIM)], axis=0)
    wcum = jnp.exp2(cum)
    inv_w = jnp.exp2(-cum)
    bt = bvec * inv_w
    kt = k2 * inv_w
    rt_ref[...] = (r * wcum).astype(BF16)
    at_ref[...] = (-kk * jnp.exp2(cum - logw)).astype(BF16)
    bt_ref[...] = bt.astype(BF16)
    kt_ref[...] = kt.astype(BF16)
    v_ref[...] = v.astype(BF16)
    for u in range(n_units):
        lo_row, hi_row = u * CHUNK, (u + 1) * CHUNK
        bkt_ref[u] = jnp.concatenate([bt[lo_row:hi_row], kt[lo_row:hi_row]], axis=0).T.astype(BF16)
        wcol_ref[u] = jnp.broadcast_to(wcum[hi_row - 1:hi_row, :], (LANES, w_)).T

    li = lax.broadcasted_iota(jnp.int32, (CHUNK, LANES), 0)
    mi = lax.broadcasted_iota(jnp.int32, (CHUNK, LANES), 1)
    lane_lo = mi < RWKV_HEAD_DIM
    mloc = jnp.where(lane_lo, mi, mi - RWKV_HEAD_DIM)
    strict = mloc < li
    incl = mloc <= li
    eye2 = (mloc == li).astype(F32)
    bi = lax.broadcasted_iota(jnp.int32, (LANES, LANES), 0)
    bj = lax.broadcasted_iota(jnp.int32, (LANES, LANES), 1)
    diag_blocks = (bi < RWKV_HEAD_DIM) == (bj < RWKV_HEAD_DIM)

    def tile_of(unit, pair):
        return (slice(unit * CHUNK, (unit + 1) * CHUNK), slice(pair * LANES, (pair + 1) * LANES))

    tiles = [tile_of(u, pr) for u in range(n_units) for pr in range(HEAD_PAIRS)]
    pw, tinv, grams, rhsl = [], [], [], []
    for rows, cols in tiles:
        at2, rt2, bt2, kt2 = (ref[rows, cols] for ref in (at_ref, rt_ref, bt_ref, kt_ref))
        grams.append(_dot_nt(jnp.concatenate([at2, rt2], axis=0),
                             jnp.concatenate([_pair_blockdiag(bt2, lane_lo), _pair_blockdiag(kt2, lane_lo)],
                                             axis=0)))
    for (rows, cols), gram in zip(tiles, grams):
        v2 = v_ref[rows, cols]
        a_ab = jnp.where(strict, gram[:CHUNK, :LANES], 0.0)
        a_ak = jnp.where(strict, gram[:CHUNK, LANES:], 0.0)
        a_rb = jnp.where(incl, gram[CHUNK:, :LANES], 0.0)
        a_rk = jnp.where(incl, gram[CHUNK:, LANES:], 0.0)
        loc = _dot(jnp.concatenate([a_ak, a_rk], axis=0).astype(BF16), _pair_blockdiag(v2, lane_lo))
        rhsl.append(loc[:CHUNK].astype(BF16))
        yl_ref[rows, cols] = loc[CHUNK:]
        arb_ref[rows, cols] = a_rb.astype(BF16)
        pw.append(a_ab)
        tinv.append(eye2 + a_ab)
    pwb = [x.astype(BF16) for x in pw]
    pw = [_dot(x, _pair_blockdiag(x, lane_lo)) for x in pwb]
    for level in range(5):
        pwb = [x.astype(BF16) for x in pw]
        if level < 4:
            both = [_dot(jnp.concatenate([x, tv.astype(BF16)], axis=0), _pair_blockdiag(x, lane_lo))
                    for x, tv in zip(pwb, tinv)]
            pw = [m[:CHUNK] for m in both]
            tinv = [tv + m[CHUNK:] for tv, m in zip(tinv, both)]
        else:
            tinv = [tv + _dot(tv.astype(BF16), _pair_blockdiag(x, lane_lo))
                    for x, tv in zip(pwb, tinv)]
    for (rows, cols), tv, x in zip(tiles, tinv, rhsl):
        both = _dot(tv.astype(BF16), jnp.concatenate([_pair_blockdiag(at_ref[rows, cols], lane_lo),
                                                      _pair_blockdiag(x, lane_lo)], axis=1))
        tat_ref[rows, cols] = both[:, :LANES].astype(BF16)
        ul_ref[rows, cols] = both[:, LANES:]

    chains = [(b, pr) for b in range(n_batch) for pr in range(HEAD_PAIRS)]
    st = [state_ref[b * HEAD_PAIRS + pr] for b, pr in chains]
    for c in range(n_chunks):
        units = [b * n_chunks + c for b, _ in chains]
        tiles = [tile_of(u, pr) for u, (_, pr) in zip(units, chains)]
        m1 = [_dot(jnp.concatenate([tat_ref[rows, cols], rt_ref[rows, cols]], axis=0), s.astype(BF16))
              for (rows, cols), s in zip(tiles, st)]
        ub = [(m[:CHUNK] + ul_ref[rows, cols]).astype(BF16) for m, (rows, cols) in zip(m1, tiles)]
        new_st = []
        for u, (_, pr), (rows, cols), s, x in zip(units, chains, tiles, st, ub):
            chans = slice(pr * LANES, (pr + 1) * LANES)
            upd = _dot(bkt_ref[u, chans, :], jnp.concatenate([x, v_ref[rows, cols]], axis=0))
            new_st.append((s + jnp.where(diag_blocks, upd, 0.0)) * wcol_ref[u, chans, :])
        st = new_st
        for m, (rows, cols), x in zip(m1, tiles, ub):
            ys_ref[rows, cols] = (m[CHUNK:] + _dot(arb_ref[rows, cols], _pair_blockdiag(x, lane_lo))
                                  + yl_ref[rows, cols])
    for (b, pr), s in zip(chains, st):
        state_ref[b * HEAD_PAIRS + pr] = s

    y = ys_ref[...]
    mean = _group_sum(y, g64_ref[...])
    yc = y - mean
    var = _group_sum(yc * yc, g64_ref[...])
    yn = yc * lax.rsqrt(var + GN_EPS) * gng_ref[...] + gnb_ref[...]
    bonus = _group_sum(r * k2 * rk_ref[...], ones64_ref[...]) * v
    y_ref[...] = ((yn + bonus) * g).astype(BF16).reshape(n_batch, tt, w_)


def _rwkv(p1, v_first, prm, use_vres):
    n_batch, seq, _ = p1.shape
    tt = TT_RWKV
    rows_all = n_batch * tt
    n_units = rows_all // CHUNK
    w_ = RWKV_WIDTH
    tok = lambda width: pl.BlockSpec((n_batch, tt, width), lambda i: (0, i, 0))
    vec = _const_spec((1, w_))
    in_specs = [tok(RWKV_EXT)]
    args = [p1]
    if use_vres:
        in_specs.append(tok(w_))
        args.append(v_first)
    in_specs += [_const_spec((1, RWKV_EXT)), vec, vec, _const_spec((LANES, 2 * w_)),
                 _const_spec((GATE_RANK, w_)), vec, vec, vec, vec, vec]
    args += [prm["mu"], prm["w0"], prm["a0"], prm["lora"], prm["gate_b"], prm["k_k"], prm["k_a"],
             prm["r_k"], prm["gn_g"], prm["gn_b"]]
    if use_vres:
        in_specs += [vec, _const_spec((LANES, w_))]
        args += [prm["v0"], prm["vres_b"]]
    in_specs += [_const_spec((MXU_DIM, MXU_DIM))] * 3
    args += [prm["g64"], prm["ones64"], prm["tri"]]
    out_specs = [tok(w_)]
    out_shape = [jax.ShapeDtypeStruct((n_batch, seq, w_), BF16)]
    if not use_vres:
        out_specs.append(tok(w_))
        out_shape.append(jax.ShapeDtypeStruct((n_batch, seq, w_), F32))
    act_bf16 = pltpu.VMEM((rows_all, w_), BF16)
    act_f32 = pltpu.VMEM((rows_all, w_), F32)
    scratch = [pltpu.VMEM((8, RWKV_EXT), F32),
               pltpu.VMEM((n_batch * HEAD_PAIRS, LANES, LANES), F32),
               act_bf16, act_bf16, act_bf16, act_bf16, act_bf16,
               pltpu.VMEM((n_units, w_, LANES), BF16),
               pltpu.VMEM((n_units, w_, LANES), F32),
               act_bf16, act_bf16, act_f32, act_f32, act_f32]
    out = pl.pallas_call(
        functools.partial(_rwkv_kernel, use_vres=use_vres, n_batch=n_batch),
        grid=(seq // tt,),
        in_specs=in_specs,
        out_specs=out_specs,
        out_shape=out_shape,
        scratch_shapes=scratch,
        compiler_params=_params(1),
        name="rwkv",
    )(*args)
    if use_vres:
        return out[0], v_first
    return out[0], out[1]


def _att_kernel(q_ref, k0_ref, k1_ref, v0_ref, v1_ref, *rest):
    n_half = TQ_ATT // TH_ATT
    bias_refs, o_ref = rest[:n_half], rest[n_half]
    lane = lax.broadcasted_iota(jnp.int32, (1, LANES), 1)
    zero = jnp.zeros((), BF16)
    head_lanes = [(lane >= sub * ATT_HEAD_DIM) & (lane < (sub + 1) * ATT_HEAD_DIM) for sub in range(2)]
    den_lane = [(1 - sub) * ATT_HEAD_DIM for sub in range(2)]
    ones_col = [jnp.where(lane == den_lane[sub], 1.0, 0.0).astype(BF16) for sub in range(2)]
    for group in range(ATT_HEADS // 2 // PAIRS_PER_STAGE):
        units = [(hp, half, sub)
                 for hp in range(group * PAIRS_PER_STAGE, (group + 1) * PAIRS_PER_STAGE)
                 for half in range(n_half) for sub in range(2)]
        k_win, v_win = {}, {}
        for hp in range(group * PAIRS_PER_STAGE, (group + 1) * PAIRS_PER_STAGE):
            cols = slice(hp * LANES, (hp + 1) * LANES)
            k_win[hp] = jnp.concatenate([k0_ref[:, cols], k1_ref[:, cols]], axis=0)
            v_win[hp] = jnp.concatenate([v0_ref[:, cols], v1_ref[:, cols]], axis=0)
        scores = []
        for hp, half, sub in units:
            q2 = q_ref[half * TH_ATT:(half + 1) * TH_ATT, hp * LANES:(hp + 1) * LANES]
            k2 = k_win[hp][half * TH_ATT:half * TH_ATT + KH_ATT]
            scores.append(_dot_nt(jnp.where(head_lanes[sub], q2, zero), k2)
                          + bias_refs[half][0, 2 * hp + sub])
        probs = [jnp.exp2(s - jnp.max(s, axis=-1, keepdims=True)).astype(BF16) for s in scores]
        pvs = []
        for (hp, half, sub), e in zip(units, probs):
            v2 = v_win[hp][half * TH_ATT:half * TH_ATT + KH_ATT]
            pvs.append(_dot(e, jnp.where(head_lanes[sub], v2, zero) + ones_col[sub]))
        for j in range(0, len(units), 2):
            hp, half, _ = units[j]
            lo, hi = pvs[j], pvs[j + 1]
            out = jnp.where(head_lanes[0], lo / lo[:, den_lane[0]:den_lane[0] + 1],
                            hi / hi[:, den_lane[1]:den_lane[1] + 1])
            o_ref[half * TH_ATT:(half + 1) * TH_ATT, hp * LANES:(hp + 1) * LANES] = out.astype(BF16)


def _att(qkv, bias, seq):
    t = qkv.shape[0]
    tq = TQ_ATT
    nq = seq // tq
    blk = lambda col, back: pl.BlockSpec(
        (tq, ATT_WIDTH), lambda b, i: (b * nq + jnp.maximum(i - back, 0), col))
    n_half = tq // TH_ATT
    bias_specs = [pl.BlockSpec((1,) + bias.shape[1:],
                               lambda b, i, half=half: (jnp.where(i == 0, half, n_half), 0, 0, 0))
                  for half in range(n_half)]
    return pl.pallas_call(
        _att_kernel,
        grid=(t // seq, nq),
        in_specs=[blk(0, 0), blk(1, 1), blk(1, 0), blk(2, 1), blk(2, 0)] + bias_specs,
        out_specs=pl.BlockSpec((tq, ATT_WIDTH), lambda b, i: (b * nq + i, 0)),
        out_shape=jax.ShapeDtypeStruct((t, ATT_WIDTH), BF16),
        compiler_params=_params(2),
        name="band_att",
    )(qkv, qkv, qkv, qkv, qkv, *([bias] * n_half))


def _att_bias(rel_table):
    heads = rel_table.shape[0]
    far = rel_table[:, N_REL - 1:]
    near = rel_table[:, :1]
    n_far = KH_ATT - TH_ATT - REL_MAX
    period = KH_ATT + TH_ATT + 1
    base = jnp.concatenate([
        jnp.broadcast_to(far, (heads, n_far)),
        rel_table[:, ::-1],
        jnp.broadcast_to(near, (heads, KH_ATT - n_far - N_REL)),
        jnp.broadcast_to(far, (heads, period - KH_ATT)),
    ], axis=1)
    rows = jnp.tile(base, (1, TH_ATT))[:, :TH_ATT * (period - 1)].reshape(heads, TH_ATT, period - 1)
    bias = rows[:, :, :KH_ATT].astype(F32) * LOG2_E
    qc = np.arange(TH_ATT)[:, None] // CHUNK
    kcol = np.arange(KH_ATT)[None, :]
    in_band = (kcol // CHUNK >= qc) & (kcol // CHUNK <= qc + LEFT_CHUNKS)
    n_half = TQ_ATT // TH_ATT
    first_valid = [max(KH_ATT - TH_ATT - half * TH_ATT, 0) for half in range(n_half)] + [0]
    keep = np.stack([in_band & (kcol >= fv) for fv in first_valid])
    return jnp.where(keep[:, None], bias[None], NEG_INF)


def _memkv_kernel(mem_ref, g_ref, w_ref, kg_ref, g128_ref, mk_ref, mv_ref):
    h = _rms(mem_ref[...], g_ref[0]).astype(BF16)
    kv = _dot(h, w_ref[0])
    mk = kv[:, :MEM_WIDTH]
    ms = _group_sum(mk * mk, g128_ref[...])
    mk_ref[0] = (mk * lax.rsqrt(ms + RMS_EPS) * kg_ref[0]).astype(BF16)
    mv_ref[0] = kv[:, MEM_WIDTH:].astype(BF16)


def _memkv(mem, norm_mem, w_kv, k_gain, g128):
    m, d = mem.shape
    depth = w_kv.shape[0]
    return pl.pallas_call(
        _memkv_kernel,
        grid=(depth,),
        in_specs=[_const_spec((m, d)),
                  pl.BlockSpec((1, 1, d), lambda l: (l, 0, 0)),
                  pl.BlockSpec((1, d, 2 * MEM_WIDTH), lambda l: (l, 0, 0)),
                  pl.BlockSpec((1, 1, MEM_WIDTH), lambda l: (l, 0, 0)),
                  _const_spec(g128.shape)],
        out_specs=[pl.BlockSpec((1, m, MEM_WIDTH), lambda l: (l, 0, 0)),
                   pl.BlockSpec((1, m, MEM_WIDTH), lambda l: (l, 0, 0))],
        out_shape=[jax.ShapeDtypeStruct((depth, m, MEM_WIDTH), BF16),
                   jax.ShapeDtypeStruct((depth, m, MEM_WIDTH), BF16)],
        compiler_params=_params(1),
        name="mem_kv",
    )(mem, norm_mem.reshape(depth, 1, d), w_kv, k_gain.reshape(depth, 1, MEM_WIDTH), g128)


def _merge_kernel(x_ref, ng_ref, yr_ref, ya_ref, ym_ref, wg_ref, bg_ref, wr_ref, wa_ref, wm_ref,
                  wo_ref, o_ref):
    x = x_ref[...]
    d = x.shape[1]
    h = _rms(x, ng_ref[...]).astype(BF16)
    gates = jax.nn.sigmoid(_dot(h, wg_ref[...]) + bg_ref[...])
    merged = (gates[:, 0:d] * _dot(yr_ref[...], wr_ref[...])
              + gates[:, d:2 * d] * _dot(ya_ref[...], wa_ref[...])
              + gates[:, 2 * d:3 * d] * _dot(ym_ref[...], wm_ref[...]))
    o_ref[...] = x + _dot(merged.astype(BF16), wo_ref[...])


def _merge(x, gain, y_rwkv, y_att, y_mem, w_gate, b_gate, w_r, w_a, w_m, w_out, layer):
    t, d = x.shape
    tm = TM_MERGE
    tok = lambda width: pl.BlockSpec((tm, width), lambda i: (i, 0))
    return pl.pallas_call(
        _merge_kernel,
        grid=(t // tm,),
        in_specs=[tok(d), _const_spec((1, d)), tok(RWKV_WIDTH), tok(ATT_WIDTH), tok(MEM_WIDTH),
                  _layer_spec(w_gate.shape, layer), _const_spec((1, 3 * d)),
                  _layer_spec(w_r.shape, layer), _layer_spec(w_a.shape, layer),
                  _layer_spec(w_m.shape, layer), _layer_spec(w_out.shape, layer)],
        out_specs=tok(d),
        out_shape=jax.ShapeDtypeStruct((t, d), F32),
        compiler_params=_params(1),
        name="merge",
    )(x, gain.reshape(1, d), y_rwkv, y_att, y_mem, w_gate, b_gate.reshape(1, 3 * d), w_r, w_a, w_m,
      w_out)


def _block_diag_const(width, group, value):
    idx = jnp.arange(width) // group
    return jnp.where(idx[:, None] == idx[None, :], value, 0.0).astype(BF16)


def _chunk_tri(rows):
    i = jnp.arange(rows)
    same_chunk = (i[:, None] // CHUNK) == (i[None, :] // CHUNK)
    return (same_chunk & (i[None, :] <= i[:, None])).astype(BF16)


def kernel(x, mem, norm_ffn1, ffn1_w_in, ffn1_w_out, norm_mix, w_in, shift_mu, decay_w0, decay_lora_b, iclr_a0, iclr_lora_b, gate_lora_b, rwkv_k_k, rwkv_k_a, rwkv_r_k, rwkv_gn_g, rwkv_gn_b, vres_v0, vres_lora_a, vres_lora_b, att_q_norm, att_k_norm, att_rel_bias, norm_mem, mem_w_kv, mem_q_norm, mem_k_norm, w_branch_rwkv, w_branch_att, w_branch_mem, w_gate, b_gate, w_out, norm_ffn2, ffn2_w_in, ffn2_w_out):
    bsz, seq, d = x.shape
    assert d == D_MODEL and all(seq % tile == 0 for tile in (TM_FFN, TM_PROJ, TM_MERGE, TT_RWKV, TQ_ATT))
    t = bsz * seq
    w_ = RWKV_WIDTH
    depth = w_in.shape[0]
    xf = x.reshape(t, d)

    g64 = _block_diag_const(MXU_DIM, RWKV_HEAD_DIM, 1.0 / RWKV_HEAD_DIM)
    ones64 = _block_diag_const(MXU_DIM, RWKV_HEAD_DIM, 1.0)
    g128 = _block_diag_const(MXU_DIM, MEM_HEAD_DIM, 1.0 / MEM_HEAD_DIM)
    tri = _chunk_tri(MXU_DIM)

    mk_all, mv_all = _memkv(mem.reshape(bsz * mem.shape[1], d), norm_mem, mem_w_kv.astype(BF16),
                            jnp.tile(mem_k_norm, (1, MEM_HEADS)), g128)

    ffn1_in, ffn1_out = ffn1_w_in.astype(BF16), ffn1_w_out.astype(BF16)
    ffn2_in, ffn2_out = ffn2_w_in.astype(BF16), ffn2_w_out.astype(BF16)
    w_proj, wg, wo = w_in.astype(BF16), w_gate.astype(BF16), w_out.astype(BF16)
    wbr, wba, wbm = w_branch_rwkv.astype(BF16), w_branch_att.astype(BF16), w_branch_mem.astype(BF16)

    v_first = None
    for l in range(depth):
        xf = _ffn(xf, norm_ffn1[l], ffn1_in, ffn1_out, l)

        use_vres = l > 0
        vres_a = vres_lora_a[l - 1] if use_vres else jnp.zeros((d, VRES_RANK), F32)
        vres_a = jnp.concatenate([vres_a, jnp.zeros((d, LANES - VRES_RANK), F32)], axis=1).astype(BF16)
        qk_gain = jnp.concatenate([jnp.tile(att_q_norm[l], ATT_HEADS) * (ATT_HEAD_DIM ** -0.5 * LOG2_E),
                                   jnp.tile(att_k_norm[l], ATT_HEADS)]).reshape(1, 2 * ATT_WIDTH)
        mq_gain = (jnp.tile(mem_q_norm[l], MEM_HEADS) * (MEM_HEAD_DIM ** -0.5)).reshape(1, MEM_WIDTH)
        p1, qkv, y_mem = _proj(xf, norm_mix[l], w_proj, vres_a, qk_gain, mq_gain, g64, g128, mk_all, mv_all,
                               l, seq)

        zeros_r = jnp.zeros((DECAY_RANK, w_), F32)
        prm = dict(
            mu=jnp.concatenate([shift_mu[l], jnp.zeros((LANES,), F32)]).reshape(1, RWKV_EXT),
            w0=decay_w0[l].reshape(1, w_), a0=iclr_a0[l].reshape(1, w_),
            lora=jnp.concatenate([jnp.concatenate([decay_lora_b[l], zeros_r], axis=1),
                                  jnp.concatenate([zeros_r, iclr_lora_b[l]], axis=1)], axis=0).astype(BF16),
            gate_b=gate_lora_b[l].astype(BF16),
            k_k=rwkv_k_k[l].reshape(1, w_), k_a=rwkv_k_a[l].reshape(1, w_),
            r_k=rwkv_r_k[l].reshape(1, w_), gn_g=rwkv_gn_g[l].reshape(1, w_),
            gn_b=rwkv_gn_b[l].reshape(1, w_), g64=g64, ones64=ones64, tri=tri)
        if use_vres:
            prm["v0"] = vres_v0[l - 1].reshape(1, w_)
            prm["vres_b"] = jnp.concatenate(
                [vres_lora_b[l - 1], jnp.zeros((LANES - VRES_RANK, w_), F32)], axis=0).astype(BF16)
        y_rwkv, v_first = _rwkv(p1.reshape(bsz, seq, RWKV_EXT), v_first, prm, use_vres)

        y_att = _att(qkv, _att_bias(att_rel_bias[l]), seq)

        xf = _merge(xf, norm_mix[l], y_rwkv.reshape(t, w_), y_att, y_mem, wg, b_gate[l], wbr, wba, wbm,
                    wo, l)
        xf = _ffn(xf, norm_ffn2[l], ffn2_in, ffn2_out, l)
    return xf.reshape(bsz, seq, d)
```
